```python
import jax, jax.numpy as jnp
from jax import lax
import numpy as np

D_MODEL = 1024
BATCH = 8
SEQ = 2048
DEPTH = 2
DEC_BATCH = 32
DEC_SEQ = 8
PAST_LEN = 16384
PAGE_SIZE = 128

HEAD_DIM = 64
MIX_WIDTH = D_MODEL
SB_HEADS = MIX_WIDTH // (2 * HEAD_DIM)
SB_WIDTH = SB_HEADS * HEAD_DIM
MLA_HEADS = MIX_WIDTH // (2 * HEAD_DIM)
MLA_NOPE = HEAD_DIM
MLA_ROPE = HEAD_DIM // 2
MLA_V = HEAD_DIM
MLA_WIDTH = MLA_HEADS * MLA_V
Q_RANK = D_MODEL // 4
KV_RANK = D_MODEL // 8
ROPE_THETA = 10000.0
D_FF = 7 * D_MODEL // 2
N_EXPERTS = 8
TOP_K = 2
D_EXPERT = 7 * D_MODEL // 2
N_DENSE = (DEPTH + 1) // 2
N_MOE = DEPTH // 2
Q_BLOCK = 128
EPS = 1e-6
IN_COLS = 3 * SB_WIDTH + Q_RANK + KV_RANK + MLA_ROPE
SB_SCALE = HEAD_DIM ** -0.5
MLA_SCALE = (MLA_NOPE + MLA_ROPE) ** -0.5

kernel_name = 'hymba_stickbreak_mla_decoder_step'


def rms_norm(x, g):
    xf = x.astype(jnp.float32)
    y = xf * lax.rsqrt(jnp.mean(xf * xf, axis=-1, keepdims=True) + EPS)
    return (y * g.astype(jnp.float32)).astype(x.dtype)


def rope(x, pos):
    half = MLA_ROPE // 2
    inv_freq = 1.0 / (ROPE_THETA ** (jnp.arange(0, MLA_ROPE, 2, dtype=jnp.float32) / MLA_ROPE))
    ang = pos.astype(jnp.float32)[:, None] * inv_freq[None, :]
    ang = ang.reshape((pos.shape[0],) + (1,) * (x.ndim - 3) + (half,))
    cos, sin = jnp.cos(ang), jnp.sin(ang)
    x1 = x[..., :half].astype(jnp.float32)
    x2 = x[..., half:].astype(jnp.float32)
    return jnp.concatenate([x1 * cos - x2 * sin, x2 * cos + x1 * sin], axis=-1).astype(x.dtype)


def stick_breaking_block(q, k, v, q_pos, k_pos):
    z = jnp.einsum('bqhd,bkhd->bhqk', q, k, preferred_element_type=jnp.float32) * SB_SCALE
    visible = k_pos[None, :] < q_pos[:, None]
    log_beta = jax.nn.log_sigmoid(z)
    log_keep = jnp.where(visible, jax.nn.log_sigmoid(-z), 0.0)
    later = lax.cumsum(log_keep, axis=3, reverse=True) - log_keep
    a = jnp.where(visible, jnp.exp(log_beta + later), 0.0)
    return jnp.einsum('bhqk,bkhd->bqhd', a.astype(v.dtype), v)


def mla_block(q_lat, q_rope, lat, k_rope, q_pos, k_pos):
    s = (jnp.einsum('bqhr,bkr->bhqk', q_lat, lat, preferred_element_type=jnp.float32)
         + jnp.einsum('bqhp,bkp->bhqk', q_rope, k_rope, preferred_element_type=jnp.float32)) * MLA_SCALE
    visible = k_pos[None, :] <= q_pos[:, None]
    p = jax.nn.softmax(jnp.where(visible, s, -jnp.inf), axis=-1)
    return jnp.einsum('bhqk,bkr->bqhr', p.astype(lat.dtype), lat)


def token_mixers(h, w_in, g_q, w_uq, g_kv, w_uk, w_uv, g_out_sb, g_out_mla, w_out, q_offset, past):
    B, S, _ = h.shape
    proj = jnp.einsum('bsd,dc->bsc', h, w_in)
    o = np.cumsum([SB_WIDTH, SB_WIDTH, SB_WIDTH, Q_RANK, KV_RANK]).tolist()
    q_sb = proj[..., :o[0]].reshape(B, S, SB_HEADS, HEAD_DIM)
    k_sb = proj[..., o[0]:o[1]].reshape(B, S, SB_HEADS, HEAD_DIM)
    v_sb = proj[..., o[1]:o[2]].reshape(B, S, SB_HEADS, HEAD_DIM)
    c_q = proj[..., o[2]:o[3]]
    c_kv = proj[..., o[3]:o[4]]
    k_rope_in = proj[..., o[4]:]
    q_pos = q_offset + jnp.arange(S, dtype=jnp.int32)
    q_mla = jnp.einsum('bsr,rhe->bshe', rms_norm(c_q, g_q), w_uq)
    q_rope = rope(q_mla[..., MLA_NOPE:], q_pos)
    q_lat = jnp.einsum('bshn,rhn->bshr', q_mla[..., :MLA_NOPE], w_uk)
    lat = rms_norm(c_kv, g_kv)
    k_rope = rope(k_rope_in, q_pos)
    new_rows = (k_sb, v_sb, lat, k_rope)
    if past is None:
        k_all, v_all, lat_all, kr_all = new_rows
    else:
        k_all, v_all, lat_all, kr_all = [jnp.concatenate([p_, n_], axis=1) for p_, n_ in zip(past, new_rows)]
    k_pos = jnp.arange(q_offset + S, dtype=jnp.int32)
    sb_out, mla_out = [], []
    for qs in range(0, S, Q_BLOCK):
        qe = min(qs + Q_BLOCK, S)
        ke = q_offset + qe
        sb_out.append(stick_breaking_block(q_sb[:, qs:qe], k_all[:, :ke], v_all[:, :ke], q_pos[qs:qe], k_pos[:ke]))
        mla_out.append(mla_block(q_lat[:, qs:qe], q_rope[:, qs:qe], lat_all[:, :ke], kr_all[:, :ke], q_pos[qs:qe], k_pos[:ke]))
    o_sb = jnp.concatenate(sb_out, axis=1).reshape(B, S, SB_WIDTH)
    o_mla = jnp.einsum('bshr,rhv->bshv', jnp.concatenate(mla_out, axis=1), w_uv).reshape(B, S, MLA_WIDTH)
    mixed = jnp.concatenate([rms_norm(o_sb, g_out_sb), rms_norm(o_mla, g_out_mla)], axis=-1)
    return jnp.einsum('bsc,cd->bsd', mixed, w_out), new_rows


def swiglu(h, wg, wu, wd):
    return jnp.einsum('bsf,fd->bsd', jax.nn.silu(jnp.einsum('bsd,df->bsf', h, wg)) * jnp.einsum('bsd,df->bsf', h, wu), wd)


def moe_swiglu(h, w_router, wg, wu, wd):
    logits = jnp.einsum('bsd,de->bse', h, w_router, preferred_element_type=jnp.float32)
    top_val, top_idx = lax.top_k(logits, TOP_K)
    gates = jax.nn.softmax(top_val, axis=-1)
    combine = jnp.sum(jax.nn.one_hot(top_idx, N_EXPERTS, dtype=jnp.float32) * gates[..., None], axis=-2)
    out = jnp.zeros_like(h)
    for e in range(N_EXPERTS):
        out = out + combine[..., e:e + 1].astype(h.dtype) * swiglu(h, wg[e], wu[e], wd[e])
    return out


def gather_pages(cache, l, page_table):
    g = cache[l, page_table]
    return g.reshape((page_table.shape[0], page_table.shape[1] * PAGE_SIZE) + cache.shape[3:])


def setup_inputs(seed: int = 0) -> dict:
    key = jax.random.key(seed)
    ks = jax.random.split(key, 32)
    n_pages = PAST_LEN // PAGE_SIZE
    n_pool = (5 * DEC_BATCH * n_pages + 3) // 4

    def nrm(k, shape, scale=1.0):
        return jax.random.normal(k, shape, jnp.float32) * scale

    def gain(k, shape):
        return 1.0 + 0.1 * jax.random.normal(k, shape, jnp.float32)

    page_table = jax.random.permutation(ks[6], n_pool)[: DEC_BATCH * n_pages].reshape(DEC_BATCH, n_pages).astype(jnp.int32)
    return {
        'x_prompt': nrm(ks[0], (BATCH, SEQ, D_MODEL)),
        'x_sample': nrm(ks[1], (DEC_BATCH, DEC_SEQ, D_MODEL)),
        'cache_sb_k': nrm(ks[2], (DEPTH, n_pool, PAGE_SIZE, SB_HEADS, HEAD_DIM)),
        'cache_sb_v': nrm(ks[3], (DEPTH, n_pool, PAGE_SIZE, SB_HEADS, HEAD_DIM)),
        'cache_mla_latent': nrm(ks[4], (DEPTH, n_pool, PAGE_SIZE, KV_RANK)),
        'cache_mla_krope': nrm(ks[5], (DEPTH, n_pool, PAGE_SIZE, MLA_ROPE)),
        'page_table': page_table,
        'g_attn_norm': gain(ks[7], (DEPTH, D_MODEL)),
        'w_in': nrm(ks[8], (DEPTH, D_MODEL, IN_COLS), D_MODEL ** -0.5),
        'g_q_norm': gain(ks[9], (DEPTH, Q_RANK)),
        'w_uq': nrm(ks[10], (DEPTH, Q_RANK, MLA_HEADS, MLA_NOPE + MLA_ROPE), Q_RANK ** -0.5),
        'g_kv_norm': gain(ks[11], (DEPTH, KV_RANK)),
        'w_uk': nrm(ks[12], (DEPTH, KV_RANK, MLA_HEADS, MLA_NOPE), KV_RANK ** -0.5),
        'w_uv': nrm(ks[13], (DEPTH, KV_RANK, MLA_HEADS, MLA_V), KV_RANK ** -0.5),
        'g_out_sb': gain(ks[14], (DEPTH, SB_WIDTH)),
        'g_out_mla': gain(ks[15], (DEPTH, MLA_WIDTH)),
        'w_out': nrm(ks[16], (DEPTH, MIX_WIDTH, D_MODEL), MIX_WIDTH ** -0.5),
        'g_ffn_norm': gain(ks[17], (DEPTH, D_MODEL)),
        'w_gate_dense': nrm(ks[18], (N_DENSE, D_MODEL, D_FF), D_MODEL ** -0.5),
        'w_up_dense': nrm(ks[19], (N_DENSE, D_MODEL, D_FF), D_MODEL ** -0.5),
        'w_down_dense': nrm(ks[20], (N_DENSE, D_FF, D_MODEL), D_FF ** -0.5),
        'w_router': nrm(ks[21], (N_MOE, D_MODEL, N_EXPERTS), D_MODEL ** -0.5),
        'w_gate_moe': nrm(ks[22], (N_MOE, N_EXPERTS, D_MODEL, D_EXPERT), D_MODEL ** -0.5),
        'w_up_moe': nrm(ks[23], (N_MOE, N_EXPERTS, D_MODEL, D_EXPERT), D_MODEL ** -0.5),
        'w_down_moe': nrm(ks[24], (N_MOE, N_EXPERTS, D_EXPERT, D_MODEL), D_EXPERT ** -0.5),
        'g_final': gain(ks[25], (D_MODEL,)),
    }


def reference(x_prompt, x_sample, cache_sb_k, cache_sb_v, cache_mla_latent, cache_mla_krope, page_table,
              g_attn_norm, w_in, g_q_norm, w_uq, g_kv_norm, w_uk, w_uv, g_out_sb, g_out_mla, w_out,
              g_ffn_norm, w_gate_dense, w_up_dense, w_down_dense, w_router, w_gate_moe, w_up_moe, w_down_moe,
              g_final):
    h_p, h_s = x_prompt, x_sample
    rows_p, rows_s = [], []
    for l in range(DEPTH):
        def layer(h, q_offset, past):
            y, rows = token_mixers(rms_norm(h, g_attn_norm[l]), w_in[l], g_q_norm[l], w_uq[l], g_kv_norm[l],
                                   w_uk[l], w_uv[l], g_out_sb[l], g_out_mla[l], w_out[l], q_offset, past)
            h = h + y
            hn = rms_norm(h, g_ffn_norm[l])
            i = l // 2
            if l % 2 == 0:
                h = h + swiglu(hn, w_gate_dense[i], w_up_dense[i], w_down_dense[i])
            else:
                h = h + moe_swiglu(hn, w_router[i], w_gate_moe[i], w_up_moe[i], w_down_moe[i])
            return h, rows

        h_p, r_p = layer(h_p, 0, None)
        past = (gather_pages(cache_sb_k, l, page_table), gather_pages(cache_sb_v, l, page_table),
                gather_pages(cache_mla_latent, l, page_table), gather_pages(cache_mla_krope, l, page_table))
        h_s, r_s = layer(h_s, past[0].shape[1], past)
        rows_p.append(r_p)
        rows_s.append(r_s)
    y_prompt = rms_norm(h_p, g_final)
    y_sample = rms_norm(h_s, g_final)
    new_sb_k_prompt = jnp.stack([r[0] for r in rows_p])
    new_sb_v_prompt = jnp.stack([r[1] for r in rows_p])
    new_latent_prompt = jnp.stack([r[2] for r in rows_p])
    new_krope_prompt = jnp.stack([r[3] for r in rows_p])
    new_sb_k_sample = jnp.stack([r[0] for r in rows_s])
    new_sb_v_sample = jnp.stack([r[1] for r in rows_s])
    new_latent_sample = jnp.stack([r[2] for r in rows_s])
    new_krope_sample = jnp.stack([r[3] for r in rows_s])
    return (y_prompt, y_sample, new_sb_k_prompt, new_sb_v_prompt, new_latent_prompt, new_krope_prompt,
            new_sb_k_sample, new_sb_v_sample, new_latent_sample, new_krope_sample)
```

```python
import functools

import jax
import jax.numpy as jnp
import numpy as np
from jax import lax
from jax.experimental import pallas as pl
from jax.experimental.pallas import tpu as pltpu

F32 = jnp.float32
BF16 = jnp.bfloat16

D_MODEL = 1024
HEAD_DIM = 64
SB_HEADS = 8
SB_WIDTH = 512
MLA_HEADS = 8
MLA_NOPE = 64
MLA_ROPE = 32
MLA_V = 64
MLA_WIDTH = 512
Q_RANK = 256
KV_RANK = 128
ROPE_THETA = 10000.0
N_EXPERTS = 8
PAGE = 128
Q_BLOCK = 128
EPS = 1e-6
SB_SCALE = HEAD_DIM ** -0.5
MLA_SCALE = (MLA_NOPE + MLA_ROPE) ** -0.5

LANES = 128
QCAT = 2 * LANES
IN_COLS_PADDED = 3 * SB_WIDTH + Q_RANK + KV_RANK + 2 * LANES
VMEM_LIMIT = 52 * 1024 * 1024
PAGES_PER_STEP = 8
NEG_INF = float("-inf")

_NT = (((1,), (1,)), ((), ()))


def _dot(a, b):
    return jnp.dot(a, b, preferred_element_type=F32)


def _dot_nt(a, b):
    return lax.dot_general(a, b, _NT, preferred_element_type=F32)


def _rms(x, g):
    return x * lax.rsqrt(jnp.mean(x * x, axis=-1, keepdims=True) + EPS) * g


def _params(sem):
    return pltpu.CompilerParams(dimension_semantics=sem, vmem_limit_bytes=VMEM_LIMIT)


def _full(shape):
    return pl.BlockSpec(shape, lambda *_: (0,) * len(shape))


def _proj_kernel(h_ref, g_ref, win_ref, gq_ref, wnope_ref, wra_ref, wrb_ref, gkv_ref, wuk_ref, cos_ref, sin_ref,
                 q_ref, k_ref, v_ref, k16_ref, v16_ref, lat_ref, kr_ref, kvcat_ref, qcat_ref):
    tm = h_ref.shape[0]
    xn = _rms(h_ref[...], g_ref[...]).astype(BF16)
    qkv = _dot(xn, win_ref[:, :3 * SB_WIDTH])
    q_ref[...] = (qkv[:, :SB_WIDTH] * SB_SCALE).astype(BF16)
    k = qkv[:, SB_WIDTH:2 * SB_WIDTH]
    v = qkv[:, 2 * SB_WIDTH:]
    k_ref[...] = k
    v_ref[...] = v
    k16_ref[...] = k.astype(BF16)
    v16_ref[...] = v.astype(BF16)

    rest = _dot(xn, win_ref[:, 3 * SB_WIDTH:])
    cos = cos_ref[...]
    sin = sin_ref[...]
    cqn = _rms(rest[:, :Q_RANK], gq_ref[...]).astype(BF16)
    lat = _rms(rest[:, Q_RANK:Q_RANK + KV_RANK], gkv_ref[...])
    o = Q_RANK + KV_RANK
    krope = rest[:, o:o + LANES] * cos + rest[:, o + LANES:] * sin
    lat_ref[...] = lat
    kr_ref[...] = krope[:, :MLA_ROPE]
    kvcat_ref[:, :LANES] = lat.astype(BF16)
    kvcat_ref[:, LANES:] = krope.astype(BF16)

    qnope = _dot(cqn, wnope_ref[...]).astype(BF16)
    ra = _dot(cqn, wra_ref[...])
    rb = _dot(cqn, wrb_ref[...])
    for p in range(MLA_HEADS // 2):
        ql = _dot(qnope[:, p * LANES:(p + 1) * LANES], wuk_ref[p]).astype(BF16)
        for s in range(2):
            hd = 2 * p + s
            qr = (ra[:, hd * LANES:(hd + 1) * LANES] * cos + rb[:, hd * LANES:(hd + 1) * LANES] * sin).astype(BF16)
            for i in range(tm // Q_BLOCK):
                rows = slice(i * Q_BLOCK, (i + 1) * Q_BLOCK)
                qcat_ref[i, hd, :, :LANES] = ql[rows, s * LANES:(s + 1) * LANES]
                qcat_ref[i, hd, :, LANES:] = qr[rows]


def _proj(h, g, w, cos_t, sin_t, tm):
    t = h.shape[0]
    row = lambda n: pl.BlockSpec((tm, n), lambda i: (i, 0))
    out_shape = (
        jax.ShapeDtypeStruct((t, SB_WIDTH), BF16),
        jax.ShapeDtypeStruct((t, SB_WIDTH), F32),
        jax.ShapeDtypeStruct((t, SB_WIDTH), F32),
        jax.ShapeDtypeStruct((t, SB_WIDTH), BF16),
        jax.ShapeDtypeStruct((t, SB_WIDTH), BF16),
        jax.ShapeDtypeStruct((t, KV_RANK), F32),
        jax.ShapeDtypeStruct((t, MLA_ROPE), F32),
        jax.ShapeDtypeStruct((t, QCAT), BF16),
        jax.ShapeDtypeStruct((t // Q_BLOCK, MLA_HEADS, Q_BLOCK, QCAT), BF16),
    )
    out_specs = (row(SB_WIDTH), row(SB_WIDTH), row(SB_WIDTH), row(SB_WIDTH), row(SB_WIDTH), row(KV_RANK),
                 row(MLA_ROPE), row(QCAT),
                 pl.BlockSpec((tm // Q_BLOCK, MLA_HEADS, Q_BLOCK, QCAT), lambda i: (i, 0, 0, 0)))
    in_specs = [row(D_MODEL), _full((1, D_MODEL)), _full((D_MODEL, IN_COLS_PADDED)), _full((1, Q_RANK)),
                _full((Q_RANK, MLA_HEADS * MLA_NOPE)), _full((Q_RANK, MLA_HEADS * LANES)),
                _full((Q_RANK, MLA_HEADS * LANES)), _full((1, KV_RANK)),
                _full((MLA_HEADS // 2, LANES, 2 * LANES)), row(LANES), row(LANES)]
    return pl.pallas_call(
        _proj_kernel, grid=(t // tm,), in_specs=in_specs, out_specs=out_specs, out_shape=out_shape,
        compiler_params=_params(("parallel",)), name="proj",
    )(h, g, w["w_in"], w["g_q"], w["w_nope"], w["w_rope_a"], w["w_rope_b"], w["g_kv"], w["w_uk_bd"], cos_t, sin_t)


def _sb_tile(z, visible):
    sp = jnp.log1p(jnp.exp(-jnp.abs(z)))
    log_beta = jnp.minimum(z, 0.0) - sp
    log_keep = log_beta - z
    if visible is not None:
        log_keep = jnp.where(visible, log_keep, 0.0)
    return log_beta, log_keep


def _suffix_sum(log_keep, upper):
    hi = log_keep.astype(BF16)
    lo = (log_keep - hi.astype(F32)).astype(BF16)
    return _dot(hi, upper) + _dot(lo, upper)


def _sb_step(qm, kt, vt, carry, acc, upper, visible):
    z = _dot_nt(qm, kt)
    log_beta, log_keep = _sb_tile(z, visible)
    later = _suffix_sum(log_keep, upper) + carry
    a = jnp.exp(log_beta + later)
    if visible is not None:
        a = jnp.where(visible, a, 0.0)
    acc = acc + _dot(a.astype(BF16), vt)
    carry = carry + jnp.sum(log_keep, axis=1, keepdims=True)
    return carry, acc


def _upper(n):
    r = lax.broadcasted_iota(jnp.int32, (n, n), 0)
    c = lax.broadcasted_iota(jnp.int32, (n, n), 1)
    return jnp.where(r > c, 1.0, 0.0).astype(BF16)


def _sb_attn_kernel(q_ref, k_ref, v_ref, o_ref):
    qi = pl.program_id(1)
    r = lax.broadcasted_iota(jnp.int32, (Q_BLOCK, LANES), 0)
    c = lax.broadcasted_iota(jnp.int32, (Q_BLOCK, LANES), 1)
    upper = _upper(LANES)
    diag_visible = c < r
    left = c < HEAD_DIM
    zero_carry = jnp.zeros((Q_BLOCK, 1), F32)
    zero_acc = jnp.zeros((Q_BLOCK, LANES), F32)
    for p in range(SB_HEADS // 2):
        cols = slice(p * LANES, (p + 1) * LANES)
        q2 = q_ref[:, cols]
        outs = []
        for s in range(2):
            qm = jnp.where(left if s == 0 else jnp.logical_not(left), q2, jnp.zeros_like(q2))

            def step(kb, st, visible, qm=qm, cols=cols):
                rows = pl.ds(pl.multiple_of(kb * Q_BLOCK, Q_BLOCK), Q_BLOCK)
                return _sb_step(qm, k_ref[rows, cols], v_ref[rows, cols], st[0], st[1], upper, visible)

            st = step(qi, (zero_carry, zero_acc), diag_visible)
            st = lax.fori_loop(0, qi, lambda j, st, step=step: step(qi - 1 - j, st, None), st)
            outs.append(st[1])
        o_ref[:, cols] = jnp.where(left, outs[0], outs[1])


def _sb_attn(q, k16, v16, batch, seq):
    nq = seq // Q_BLOCK
    return pl.pallas_call(
        _sb_attn_kernel, grid=(batch, nq),
        in_specs=[pl.BlockSpec((Q_BLOCK, SB_WIDTH), lambda b, i: (b * nq + i, 0)),
                  pl.BlockSpec((seq, SB_WIDTH), lambda b, i: (b, 0)),
                  pl.BlockSpec((seq, SB_WIDTH), lambda b, i: (b, 0))],
        out_specs=pl.BlockSpec((Q_BLOCK, SB_WIDTH), lambda b, i: (b * nq + i, 0)),
        out_shape=jax.ShapeDtypeStruct((batch * seq, SB_WIDTH), F32),
        compiler_params=_params(("parallel", "parallel")), name="sb_attn",
    )(q, k16, v16)


def _mla_attn_kernel(q_ref, kv_ref, o_ref, m_ref, l_ref, acc_ref):
    qi = pl.program_id(1)
    rows = MLA_HEADS * Q_BLOCK
    q = q_ref[0].reshape(rows, QCAT)
    r = lax.broadcasted_iota(jnp.int32, (rows, Q_BLOCK), 0) % Q_BLOCK
    c = lax.broadcasted_iota(jnp.int32, (rows, Q_BLOCK), 1)

    def scores(kb):
        kv = kv_ref[pl.ds(pl.multiple_of(kb * Q_BLOCK, Q_BLOCK), Q_BLOCK), :]
        return _dot_nt(q, kv) * MLA_SCALE, kv[:, :KV_RANK]

    s, lat = scores(qi)
    s = jnp.where(c <= r, s, NEG_INF)
    m = jnp.max(s, axis=1, keepdims=True)
    p = jnp.exp(s - m)
    m_ref[...] = m
    l_ref[...] = jnp.sum(p, axis=1, keepdims=True)
    acc_ref[...] = _dot(p.astype(BF16), lat)

    @pl.loop(0, qi)
    def _(kb):
        s, lat = scores(kb)
        m_prev = m_ref[...]
        m_new = jnp.maximum(m_prev, jnp.max(s, axis=1, keepdims=True))
        alpha = jnp.exp(m_prev - m_new)
        p = jnp.exp(s - m_new)
        l_ref[...] = alpha * l_ref[...] + jnp.sum(p, axis=1, keepdims=True)
        acc_ref[...] = alpha * acc_ref[...] + _dot(p.astype(BF16), lat)
        m_ref[...] = m_new

    out = (acc_ref[...] / l_ref[...]).astype(BF16)
    for hd in range(MLA_HEADS):
        o_ref[:, hd * KV_RANK:(hd + 1) * KV_RANK] = out[hd * Q_BLOCK:(hd + 1) * Q_BLOCK]


def _mla_attn(qcat, kvcat, batch, seq):
    nq = seq // Q_BLOCK
    rows = MLA_HEADS * Q_BLOCK
    return pl.pallas_call(
        _mla_attn_kernel, grid=(batch, nq),
        in_specs=[pl.BlockSpec((1, MLA_HEADS, Q_BLOCK, QCAT), lambda b, i: (b * nq + i, 0, 0, 0)),
                  pl.BlockSpec((seq, QCAT), lambda b, i: (b, 0))],
        out_specs=pl.BlockSpec((Q_BLOCK, MLA_HEADS * KV_RANK), lambda b, i: (b * nq + i, 0)),
        out_shape=jax.ShapeDtypeStruct((batch * seq, MLA_HEADS * KV_RANK), BF16),
        scratch_shapes=[pltpu.VMEM((rows, 1), F32), pltpu.VMEM((rows, 1), F32), pltpu.VMEM((rows, KV_RANK), F32)],
        compiler_params=_params(("parallel", "parallel")), name="mla_attn",
    )(qcat, kvcat)


def _decode_kernel(pt_ref, qsb_ref, knew_ref, vnew_ref, qcat_ref, latnew_ref, krnew_ref, *refs):
    n = PAGES_PER_STEP
    k_refs, v_refs, lat_refs, kr_refs = refs[:n], refs[n:2 * n], refs[2 * n:3 * n], refs[3 * n:4 * n]
    osb_ref, omla_ref = refs[4 * n:4 * n + 2]
    acc_sb, carry_ref, m_ref, l_ref, acc_mla = refs[4 * n + 2:]
    step = pl.program_id(1)
    nq = qsb_ref.shape[0]
    rows = SB_HEADS * nq

    q_tiled = jnp.concatenate([qsb_ref[...]] * SB_HEADS, axis=0)
    row_head = lax.broadcasted_iota(jnp.int32, (rows, SB_WIDTH), 0) // nq
    lane_head = lax.broadcasted_iota(jnp.int32, (rows, SB_WIDTH), 1) // HEAD_DIM
    own_head = row_head == lane_head
    q_bd = jnp.where(own_head, q_tiled, 0.0).astype(BF16)
    qcat = qcat_ref[...].astype(BF16)
    q_lat = qcat[:, :KV_RANK]
    q_rope = qcat[:, KV_RANK:KV_RANK + MLA_ROPE]
    upper = _upper(PAGE)
    q_idx = lax.broadcasted_iota(jnp.int32, (rows, PAGE), 0) % nq
    key_idx = lax.broadcasted_iota(jnp.int32, (rows, PAGE), 1)

    def sb_block(kpage, vpage, visible):
        carry, acc = _sb_step(q_bd, kpage.astype(BF16), vpage.astype(BF16), carry_ref[...], acc_sb[...], upper,
                              visible)
        carry_ref[...] = carry
        acc_sb[...] = acc

    def mla_scores(lat, kr):
        return (_dot_nt(q_lat, lat) + _dot_nt(q_rope, kr)) * MLA_SCALE

    @pl.when(step == 0)
    def _():
        pad = PAGE - nq
        carry_ref[...] = jnp.zeros_like(carry_ref)
        acc_sb[...] = jnp.zeros_like(acc_sb)
        knew = jnp.concatenate([knew_ref[...], jnp.zeros((pad, SB_WIDTH), F32)], axis=0)
        vnew = jnp.concatenate([vnew_ref[...], jnp.zeros((pad, SB_WIDTH), F32)], axis=0)
        sb_block(knew, vnew, key_idx < q_idx)
        lat = jnp.concatenate([latnew_ref[...], jnp.zeros((pad, KV_RANK), F32)], axis=0).astype(BF16)
        kr = jnp.concatenate([krnew_ref[...], jnp.zeros((pad, MLA_ROPE), F32)], axis=0).astype(BF16)
        s = jnp.where(key_idx <= q_idx, mla_scores(lat, kr), NEG_INF)
        m = jnp.max(s, axis=1, keepdims=True)
        p = jnp.exp(s - m)
        m_ref[...] = m
        l_ref[...] = jnp.sum(p, axis=1, keepdims=True)
        acc_mla[...] = _dot(p.astype(BF16), lat)

    for j in range(n):
        sb_block(k_refs[j][...], v_refs[j][...], None)
        lat = lat_refs[j][...].astype(BF16)
        s = mla_scores(lat, kr_refs[j][...].astype(BF16))
        m_prev = m_ref[...]
        m_new = jnp.maximum(m_prev, jnp.max(s, axis=1, keepdims=True))
        alpha = jnp.exp(m_prev - m_new)
        p = jnp.exp(s - m_new)
        l_ref[...] = alpha * l_ref[...] + jnp.sum(p, axis=1, keepdims=True)
        acc_mla[...] = alpha * acc_mla[...] + _dot(p.astype(BF16), lat)
        m_ref[...] = m_new

    @pl.when(step == pl.num_programs(1) - 1)
    def _():
        acc = jnp.where(own_head, acc_sb[...], 0.0)
        out = acc[:nq]
        for hd in range(1, SB_HEADS):
            out = out + acc[hd * nq:(hd + 1) * nq]
        osb_ref[...] = out
        omla_ref[...] = acc_mla[...] / l_ref[...]


def _decode(layer, page_table, qsb, knew, vnew, qcat, latnew, krnew, ck, cv, clat, ckr):
    nb, nq, _ = qsb.shape
    n_pages = page_table.shape[1]
    n = PAGES_PER_STEP
    steps = n_pages // n
    rows = SB_HEADS * nq

    def per_seq(shape):
        return pl.BlockSpec((None,) + shape, lambda b, s, pt: (b, 0, 0))

    def page_spec(width, j):
        return pl.BlockSpec((None, None, PAGE, width),
                            lambda b, s, pt, j=j: (layer, pt[b, n_pages - 1 - (s * n + j)], 0, 0))

    in_specs = [per_seq((nq, SB_WIDTH)), per_seq((nq, SB_WIDTH)), per_seq((nq, SB_WIDTH)), per_seq((rows, QCAT)),
                per_seq((nq, KV_RANK)), per_seq((nq, MLA_ROPE))]
    for width in (SB_WIDTH, SB_WIDTH, KV_RANK, MLA_ROPE):
        in_specs += [page_spec(width, j) for j in range(n)]
    grid_spec = pltpu.PrefetchScalarGridSpec(
        num_scalar_prefetch=1, grid=(nb, steps), in_specs=in_specs,
        out_specs=(per_seq((nq, SB_WIDTH)), per_seq((rows, KV_RANK))),
        scratch_shapes=[pltpu.VMEM((rows, SB_WIDTH), F32), pltpu.VMEM((rows, 1), F32), pltpu.VMEM((rows, 1), F32),
                        pltpu.VMEM((rows, 1), F32), pltpu.VMEM((rows, KV_RANK), F32)])
    return pl.pallas_call(
        _decode_kernel, grid_spec=grid_spec,
        out_shape=(jax.ShapeDtypeStruct((nb, nq, SB_WIDTH), F32), jax.ShapeDtypeStruct((nb, rows, KV_RANK), F32)),
        compiler_params=_params(("parallel", "arbitrary")), name="decode_attn",
    )(page_table, qsb, knew, vnew, qcat, latnew, krnew, *([ck] * n), *([cv] * n), *([clat] * n), *([ckr] * n))


def _post_kernel(osb_ref, omla_ref, h_ref, gsb_ref, gmla_ref, wuv_ref, wout_ref, gffn_ref, h1_ref, hn_ref):
    pairs = [_dot(omla_ref[:, p * 2 * KV_RANK:(p + 1) * 2 * KV_RANK], wuv_ref[p]) for p in range(MLA_HEADS // 2)]
    o_mla = jnp.concatenate(pairs, axis=1)
    mixed = jnp.concatenate([_rms(osb_ref[...], gsb_ref[...]), _rms(o_mla, gmla_ref[...])], axis=1).astype(BF16)
    h1 = h_ref[...] + _dot(mixed, wout_ref[...])
    h1_ref[...] = h1
    hn_ref[...] = _rms(h1, gffn_ref[...]).astype(BF16)


def _post(osb, omla, h, w, tm):
    t = h.shape[0]
    row = lambda n: pl.BlockSpec((tm, n), lambda i: (i, 0))
    return pl.pallas_call(
        _post_kernel, grid=(t // tm,),
        in_specs=[row(SB_WIDTH), row(MLA_HEADS * KV_RANK), row(D_MODEL), _full((1, SB_WIDTH)), _full((1, MLA_WIDTH)),
                  _full((MLA_HEADS // 2, 2 * KV_RANK, 2 * MLA_V)), _full((D_MODEL, D_MODEL)), _full((1, D_MODEL))],
        out_specs=(row(D_MODEL), row(D_MODEL)),
        out_shape=(jax.ShapeDtypeStruct((t, D_MODEL), F32), jax.ShapeDtypeStruct((t, D_MODEL), BF16)),
        compiler_params=_params(("parallel",)), name="post_attn",
    )(osb, omla, h, w["g_out_sb"], w["g_out_mla"], w["w_uv_bd"], w["w_out"], w["g_ffn"])


def _swiglu_chunk(x, wg, wu, wd):
    g = _dot(x, wg)
    act = (g * jax.nn.sigmoid(g) * _dot(x, wu)).astype(BF16)
    return _dot(act, wd)


def _ffn_kernel(hn_ref, h1_ref, wg_ref, wu_ref, wd_ref, gfin_ref, o_ref, acc_ref, *, final_norm):
    j = pl.program_id(1)

    @pl.when(j == 0)
    def _():
        acc_ref[...] = jnp.zeros_like(acc_ref)

    acc_ref[...] += _swiglu_chunk(hn_ref[...], wg_ref[...], wu_ref[...], wd_ref[...])

    @pl.when(j == pl.num_programs(1) - 1)
    def _():
        out = h1_ref[...] + acc_ref[...]
        o_ref[...] = _rms(out, gfin_ref[...]) if final_norm else out


def _ffn(hn, h1, wg, wu, wd, g_final, tm, tf, final_norm):
    t = hn.shape[0]
    f = wg.shape[1]
    return pl.pallas_call(
        functools.partial(_ffn_kernel, final_norm=final_norm), grid=(t // tm, f // tf),
        in_specs=[pl.BlockSpec((tm, D_MODEL), lambda i, j: (i, 0)), pl.BlockSpec((tm, D_MODEL), lambda i, j: (i, 0)),
                  pl.BlockSpec((D_MODEL, tf), lambda i, j: (0, j)), pl.BlockSpec((D_MODEL, tf), lambda i, j: (0, j)),
                  pl.BlockSpec((tf, D_MODEL), lambda i, j: (j, 0)), _full((1, D_MODEL))],
        out_specs=pl.BlockSpec((tm, D_MODEL), lambda i, j: (i, 0)),
        out_shape=jax.ShapeDtypeStruct((t, D_MODEL), F32),
        scratch_shapes=[pltpu.VMEM((tm, D_MODEL), F32)],
        compiler_params=_params(("parallel", "arbitrary")), name="ffn_dense",
    )(hn, h1, wg, wu, wd, g_final)


def _router_kernel(h1_ref, gffn_ref, wr_ref, comb_ref):
    hn = _rms(h1_ref[...], gffn_ref[...])
    logits = jnp.dot(hn, wr_ref[...], precision=lax.Precision.HIGHEST, preferred_element_type=F32)
    lane = lax.broadcasted_iota(jnp.int32, logits.shape, 1)
    logits = jnp.where(lane < N_EXPERTS, logits, NEG_INF)
    m1 = jnp.max(logits, axis=1, keepdims=True)
    i1 = jnp.min(jnp.where(logits == m1, lane, LANES), axis=1, keepdims=True)
    rest = jnp.where(lane == i1, NEG_INF, logits)
    m2 = jnp.max(rest, axis=1, keepdims=True)
    i2 = jnp.min(jnp.where(rest == m2, lane, LANES), axis=1, keepdims=True)
    e2 = jnp.exp(m2 - m1)
    g1 = 1.0 / (1.0 + e2)
    g2 = e2 / (1.0 + e2)
    comb_ref[...] = jnp.where(lane == i1, g1, 0.0) + jnp.where(lane == i2, g2, 0.0)


def _router(h1, g_ffn, w_router, tm):
    t = h1.shape[0]
    return pl.pallas_call(
        _router_kernel, grid=(t // tm,),
        in_specs=[pl.BlockSpec((tm, D_MODEL), lambda i: (i, 0)), _full((1, D_MODEL)), _full((D_MODEL, LANES))],
        out_specs=pl.BlockSpec((tm, LANES), lambda i: (i, 0)),
        out_shape=jax.ShapeDtypeStruct((t, LANES), F32),
        compiler_params=_params(("parallel",)), name="router",
    )(h1, g_ffn, w_router)


def _moe_kernel(hn_ref, h1_ref, comb_ref, wg_ref, wu_ref, wd_ref, gfin_ref, o_ref, acc_ref, *, final_norm):
    e = pl.program_id(1)
    j = pl.program_id(2)

    @pl.when(jnp.logical_and(e == 0, j == 0))
    def _():
        acc_ref[...] = jnp.zeros_like(acc_ref)

    lane = lax.broadcasted_iota(jnp.int32, comb_ref.shape, 1)
    gate = jnp.sum(jnp.where(lane == e, comb_ref[...], 0.0), axis=1, keepdims=True)
    acc_ref[...] += gate * _swiglu_chunk(hn_ref[...], wg_ref[...], wu_ref[...], wd_ref[...])

    @pl.when(jnp.logical_and(e == pl.num_programs(1) - 1, j == pl.num_programs(2) - 1))
    def _():
        out = h1_ref[...] + acc_ref[...]
        o_ref[...] = _rms(out, gfin_ref[...]) if final_norm else out


def _moe(hn, h1, comb, wg, wu, wd, g_final, tm, tf, final_norm):
    t = hn.shape[0]
    ne, _, f = wg.shape
    tok = lambda n: pl.BlockSpec((tm, n), lambda i, e, j: (i, 0))
    return pl.pallas_call(
        functools.partial(_moe_kernel, final_norm=final_norm), grid=(t // tm, ne, f // tf),
        in_specs=[tok(D_MODEL), tok(D_MODEL), tok(LANES),
                  pl.BlockSpec((None, D_MODEL, tf), lambda i, e, j: (e, 0, j)),
                  pl.BlockSpec((None, D_MODEL, tf), lambda i, e, j: (e, 0, j)),
                  pl.BlockSpec((None, tf, D_MODEL), lambda i, e, j: (e, j, 0)), _full((1, D_MODEL))],
        out_specs=tok(D_MODEL),
        out_shape=jax.ShapeDtypeStruct((t, D_MODEL), F32),
        scratch_shapes=[pltpu.VMEM((tm, D_MODEL), F32)],
        compiler_params=_params(("parallel", "arbitrary", "arbitrary")), name="ffn_moe",
    )(hn, h1, comb, wg, wu, wd, g_final)


def _prep_layer(l, g_attn_norm, w_in, g_q_norm, w_uq, g_kv_norm, w_uk, w_uv, g_out_sb, g_out_mla, w_out, g_ffn_norm):
    half = MLA_ROPE // 2
    o = 3 * SB_WIDTH + Q_RANK + KV_RANK
    kr = w_in[l][:, o:]
    pad = jnp.zeros((D_MODEL, LANES - MLA_ROPE), F32)
    kr_a = jnp.concatenate([kr, pad], axis=1)
    kr_b = jnp.concatenate([-kr[:, half:], kr[:, :half], pad], axis=1)
    w_in_p = jnp.concatenate([w_in[l][:, :o], kr_a, kr_b], axis=1).astype(BF16)

    uq = w_uq[l]
    w_nope = uq[:, :, :MLA_NOPE].reshape(Q_RANK, MLA_HEADS * MLA_NOPE).astype(BF16)
    qr = uq[:, :, MLA_NOPE:]
    hpad = jnp.zeros((Q_RANK, MLA_HEADS, LANES - MLA_ROPE), F32)
    w_rope_a = jnp.concatenate([qr, hpad], axis=2).reshape(Q_RANK, MLA_HEADS * LANES).astype(BF16)
    w_rope_b = jnp.concatenate([-qr[:, :, half:], qr[:, :, :half], hpad], axis=2)
    w_rope_b = w_rope_b.reshape(Q_RANK, MLA_HEADS * LANES).astype(BF16)

    uk = jnp.transpose(w_uk[l], (1, 2, 0)).reshape(MLA_HEADS // 2, 2, MLA_NOPE, KV_RANK)
    zk = jnp.zeros((MLA_HEADS // 2, MLA_NOPE, KV_RANK), F32)
    w_uk_bd = jnp.concatenate([jnp.concatenate([uk[:, 0], zk], axis=2),
                               jnp.concatenate([zk, uk[:, 1]], axis=2)], axis=1).astype(BF16)
    uv = jnp.transpose(w_uv[l], (1, 0, 2)).reshape(MLA_HEADS // 2, 2, KV_RANK, MLA_V)
    zv = jnp.zeros((MLA_HEADS // 2, KV_RANK, MLA_V), F32)
    w_uv_bd = jnp.concatenate([jnp.concatenate([uv[:, 0], zv], axis=2),
                               jnp.concatenate([zv, uv[:, 1]], axis=2)], axis=1).astype(BF16)
    return {
        "g_attn": g_attn_norm[l][None], "w_in": w_in_p, "g_q": g_q_norm[l][None], "w_nope": w_nope,
        "w_rope_a": w_rope_a, "w_rope_b": w_rope_b, "g_kv": g_kv_norm[l][None], "w_uk_bd": w_uk_bd,
        "w_uv_bd": w_uv_bd, "g_out_sb": g_out_sb[l][None], "g_out_mla": g_out_mla[l][None],
        "w_out": w_out[l].astype(BF16), "g_ffn": g_ffn_norm[l][None],
    }


def _rope_tables(pos):
    inv_freq = 1.0 / (ROPE_THETA ** (jnp.arange(0, MLA_ROPE, 2, dtype=F32) / MLA_ROPE))
    ang = pos.astype(F32)[:, None] * inv_freq[None, :]
    pad = jnp.zeros((pos.shape[0], LANES - MLA_ROPE), F32)
    cos, sin = jnp.cos(ang), jnp.sin(ang)
    return jnp.concatenate([cos, cos, pad], axis=1), jnp.concatenate([sin, sin, pad], axis=1)


def kernel(x_prompt, x_sample, cache_sb_k, cache_sb_v, cache_mla_latent, cache_mla_krope, page_table, g_attn_norm,
           w_in, g_q_norm, w_uq, g_kv_norm, w_uk, w_uv, g_out_sb, g_out_mla, w_out, g_ffn_norm, w_gate_dense,
           w_up_dense, w_down_dense, w_router, w_gate_moe, w_up_moe, w_down_moe, g_final):
    batch, seq, _ = x_prompt.shape
    nb, nq, _ = x_sample.shape
    depth, n_pool = cache_sb_k.shape[:2]
    past_len = page_table.shape[1] * PAGE
    tp, ts = batch * seq, nb * nq

    ck = cache_sb_k.reshape(depth, n_pool, PAGE, SB_WIDTH)
    cv = cache_sb_v.reshape(depth, n_pool, PAGE, SB_WIDTH)
    cos_p, sin_p = _rope_tables(jnp.tile(jnp.arange(seq, dtype=jnp.int32), batch))
    cos_s, sin_s = _rope_tables(jnp.tile(past_len + jnp.arange(nq, dtype=jnp.int32), nb))
    g_fin = g_final[None]

    h_p = x_prompt.reshape(tp, D_MODEL)
    h_s = x_sample.reshape(ts, D_MODEL)
    rows_p, rows_s = [], []
    for l in range(depth):
        w = _prep_layer(l, g_attn_norm, w_in, g_q_norm, w_uq, g_kv_norm, w_uk, w_uv, g_out_sb, g_out_mla, w_out,
                        g_ffn_norm)
        last = l == depth - 1

        q, k, v, k16, v16, lat, kr, kvcat, qcat = _proj(h_p, w["g_attn"], w, cos_p, sin_p, 512)
        rows_p.append((k, v, lat, kr))
        o_sb = _sb_attn(q, k16, v16, batch, seq)
        o_mla = _mla_attn(qcat, kvcat, batch, seq)
        h1_p, hn_p = _post(o_sb, o_mla, h_p, w, 512)

        q, k, v, _, _, lat, kr, _, qcat = _proj(h_s, w["g_attn"], w, cos_s, sin_s, ts)
        rows_s.append((k, v, lat, kr))
        qcat_s = qcat.reshape(ts // Q_BLOCK, MLA_HEADS, Q_BLOCK // nq, nq, QCAT)
        qcat_s = jnp.transpose(qcat_s, (0, 2, 1, 3, 4)).reshape(nb, MLA_HEADS * nq, QCAT).astype(F32)
        o_sb, o_mla = _decode(l, page_table, q.astype(F32).reshape(nb, nq, SB_WIDTH), k.reshape(nb, nq, SB_WIDTH),
                              v.reshape(nb, nq, SB_WIDTH), qcat_s, lat.reshape(nb, nq, KV_RANK),
                              kr.reshape(nb, nq, MLA_ROPE), ck, cv, cache_mla_latent, cache_mla_krope)
        o_mla = jnp.transpose(o_mla.reshape(nb, MLA_HEADS, nq, KV_RANK), (0, 2, 1, 3))
        o_mla = o_mla.reshape(ts, MLA_HEADS * KV_RANK).astype(BF16)
        h1_s, hn_s = _post(o_sb.reshape(ts, SB_WIDTH), o_mla, h_s, w, ts)

        i = l // 2
        if l % 2 == 0:
            wg, wu, wd = w_gate_dense[i].astype(BF16), w_up_dense[i].astype(BF16), w_down_dense[i].astype(BF16)
            h_p = _ffn(hn_p, h1_p, wg, wu, wd, g_fin, 1024, 512, last)
            h_s = _ffn(hn_s, h1_s, wg, wu, wd, g_fin, ts, 512, last)
        else:
            wg, wu, wd = w_gate_moe[i].astype(BF16), w_up_moe[i].astype(BF16), w_down_moe[i].astype(BF16)
            wr = jnp.concatenate([w_router[i], jnp.zeros((D_MODEL, LANES - N_EXPERTS), F32)], axis=1)
            h_p = _moe(hn_p, h1_p, _router(h1_p, w["g_ffn"], wr, 1024), wg, wu, wd, g_fin, 1024, 512, last)
            h_s = _moe(hn_s, h1_s, _router(h1_s, w["g_ffn"], wr, ts), wg, wu, wd, g_fin, ts, 512, last)

    y_prompt = h_p.reshape(batch, seq, D_MODEL)
    y_sample = h_s.reshape(nb, nq, D_MODEL)

    def stack(rows, idx, shape):
        return jnp.stack([r[idx].reshape(shape) for r in rows])

    return (y_prompt, y_sample,
            stack(rows_p, 0, (batch, seq, SB_HEADS, HEAD_DIM)), stack(rows_p, 1, (batch, seq, SB_HEADS, HEAD_DIM)),
            stack(rows_p, 2, (batch, seq, KV_RANK)), stack(rows_p, 3, (batch, seq, MLA_ROPE)),
            stack(rows_s, 0, (nb, nq, SB_HEADS, HEAD_DIM)), stack(rows_s, 1, (nb, nq, SB_HEADS, HEAD_DIM)),
            stack(rows_s, 2, (nb, nq, KV_RANK)), stack(rows_s, 3, (nb, nq, MLA_ROPE)))
```

```python
import functools

import jax
import jax.numpy as jnp
from jax import lax
from jax.experimental import pallas as pl
from jax.experimental.pallas import tpu as pltpu

F32 = jnp.float32
BF16 = jnp.bfloat16

D_MODEL = 1024
HEAD_DIM = 64
SB_HEADS = 8
SB_WIDTH = 512
MLA_HEADS = 8
MLA_NOPE = 64
MLA_ROPE = 32
MLA_V = 64
MLA_WIDTH = 512
Q_RANK = 256
KV_RANK = 128
ROPE_THETA = 10000.0
N_EXPERTS = 8
PAGE = 128
Q_BLOCK = 128
EPS = 1e-6
SB_SCALE = HEAD_DIM ** -0.5
MLA_SCALE = (MLA_NOPE + MLA_ROPE) ** -0.5
LOG2E = 1.4426950408889634

LANES = 128
QCAT = 2 * LANES
IN_COLS_PADDED = 3 * SB_WIDTH + Q_RANK + KV_RANK + 2 * LANES
VMEM_LIMIT = 52 * 1024 * 1024
PAGES_PER_STEP = 16
NEG_INF = float("-inf")

_NT = (((1,), (1,)), ((), ()))


def _dot(a, b):
    return jnp.dot(a, b, preferred_element_type=F32)


def _dot_nt(a, b):
    return lax.dot_general(a, b, _NT, preferred_element_type=F32)


def _rms(x, g):
    return x * lax.rsqrt(jnp.mean(x * x, axis=-1, keepdims=True) + EPS) * g


def _params(sem):
    return pltpu.CompilerParams(dimension_semantics=sem, vmem_limit_bytes=VMEM_LIMIT)


def _full(shape):
    return pl.BlockSpec(shape, lambda *_: (0,) * len(shape))


def _proj_kernel(h_ref, g_ref, win_ref, gq_ref, wnope_ref, wra_ref, wrb_ref, gkv_ref, wuk_ref, cos_ref, sin_ref,
                 q_ref, k_ref, v_ref, k16_ref, v16_ref, lat_ref, kr_ref, kvcat_ref, latt_ref, qcat_ref):
    tm = h_ref.shape[0]
    xn = _rms(h_ref[...], g_ref[...]).astype(BF16)
    qkv = _dot(xn, win_ref[:, :3 * SB_WIDTH])
    q_ref[...] = (qkv[:, :SB_WIDTH] * SB_SCALE).astype(BF16)
    k = qkv[:, SB_WIDTH:2 * SB_WIDTH]
    v = qkv[:, 2 * SB_WIDTH:]
    k_ref[...] = k
    v_ref[...] = v
    k16_ref[...] = k.astype(BF16)
    v16_ref[...] = v.astype(BF16)

    rest = _dot(xn, win_ref[:, 3 * SB_WIDTH:])
    cos = cos_ref[...]
    sin = sin_ref[...]
    cqn = _rms(rest[:, :Q_RANK], gq_ref[...]).astype(BF16)
    lat = _rms(rest[:, Q_RANK:Q_RANK + KV_RANK], gkv_ref[...])
    o = Q_RANK + KV_RANK
    krope = rest[:, o:o + LANES] * cos + rest[:, o + LANES:] * sin
    lat_ref[...] = lat
    kr_ref[...] = krope[:, :MLA_ROPE]
    kvcat_ref[:, :LANES] = lat.astype(BF16)
    kvcat_ref[:, LANES:] = krope.astype(BF16)
    for i in range(tm // Q_BLOCK):
        latt_ref[i] = lat[i * Q_BLOCK:(i + 1) * Q_BLOCK].T.astype(BF16)

    qnope = _dot(cqn, wnope_ref[...]).astype(BF16)
    ra = _dot(cqn, wra_ref[...])
    rb = _dot(cqn, wrb_ref[...])
    for p in range(MLA_HEADS // 2):
        ql = _dot(qnope[:, p * LANES:(p + 1) * LANES], wuk_ref[p]).astype(BF16)
        for s in range(2):
            hd = 2 * p + s
            qr = (ra[:, hd * LANES:(hd + 1) * LANES] * cos + rb[:, hd * LANES:(hd + 1) * LANES] * sin).astype(BF16)
            for i in range(tm // Q_BLOCK):
                rows = slice(i * Q_BLOCK, (i + 1) * Q_BLOCK)
                qcat_ref[i, hd, :, :LANES] = ql[rows, s * LANES:(s + 1) * LANES]
                qcat_ref[i, hd, :, LANES:] = qr[rows]


def _proj(h, g, w, cos_t, sin_t, tm):
    t = h.shape[0]
    row = lambda n: pl.BlockSpec((tm, n), lambda i: (i, 0))
    out_shape = (
        jax.ShapeDtypeStruct((t, SB_WIDTH), BF16),
        jax.ShapeDtypeStruct((t, SB_WIDTH), F32),
        jax.ShapeDtypeStruct((t, SB_WIDTH), F32),
        jax.ShapeDtypeStruct((t, SB_WIDTH), BF16),
        jax.ShapeDtypeStruct((t, SB_WIDTH), BF16),
        jax.ShapeDtypeStruct((t, KV_RANK), F32),
        jax.ShapeDtypeStruct((t, MLA_ROPE), F32),
        jax.ShapeDtypeStruct((t, QCAT), BF16),
        jax.ShapeDtypeStruct((t // Q_BLOCK, KV_RANK, Q_BLOCK), BF16),
        jax.ShapeDtypeStruct((t // Q_BLOCK, MLA_HEADS, Q_BLOCK, QCAT), BF16),
    )
    out_specs = (row(SB_WIDTH), row(SB_WIDTH), row(SB_WIDTH), row(SB_WIDTH), row(SB_WIDTH), row(KV_RANK),
                 row(MLA_ROPE), row(QCAT),
                 pl.BlockSpec((tm // Q_BLOCK, KV_RANK, Q_BLOCK), lambda i: (i, 0, 0)),
                 pl.BlockSpec((tm // Q_BLOCK, MLA_HEADS, Q_BLOCK, QCAT), lambda i: (i, 0, 0, 0)))
    in_specs = [row(D_MODEL), _full((1, D_MODEL)), _full((D_MODEL, IN_COLS_PADDED)), _full((1, Q_RANK)),
                _full((Q_RANK, MLA_HEADS * MLA_NOPE)), _full((Q_RANK, MLA_HEADS * LANES)),
                _full((Q_RANK, MLA_HEADS * LANES)), _full((1, KV_RANK)),
                _full((MLA_HEADS // 2, LANES, 2 * LANES)), row(LANES), row(LANES)]
    return pl.pallas_call(
        _proj_kernel, grid=(t // tm,), in_specs=in_specs, out_specs=out_specs, out_shape=out_shape,
        compiler_params=_params(("parallel",)), name="proj",
    )(h, g, w["w_in"], w["g_q"], w["w_nope"], w["w_rope_a"], w["w_rope_b"], w["g_kv"], w["w_uk_bd"], cos_t, sin_t)


def _upper_ones(n):
    r = lax.broadcasted_iota(jnp.int32, (2 * n, 2 * n), 0) % n
    c = lax.broadcasted_iota(jnp.int32, (2 * n, 2 * n), 1)
    return jnp.where(jnp.logical_or(r > c, c >= n), 1.0, 0.0).astype(BF16)


def _sb_log_terms(z, upper_ones, visible):
    n = z.shape[1]
    sp = jnp.log(1.0 + jnp.exp2(jnp.abs(z) * (-LOG2E)))
    log_beta = jnp.minimum(z, 0.0) - sp
    log_keep = log_beta - z
    if visible is not None:
        log_keep = jnp.where(visible, log_keep, 0.0)
    hi = log_keep.astype(BF16)
    lo = (log_keep - hi.astype(F32)).astype(BF16)
    sums = _dot(jnp.concatenate([hi, lo], axis=1), upper_ones)
    return log_beta, sums[:, :n], sums[:, n:]


def _sb_weights(log_beta, later, visible):
    a = jnp.exp(log_beta + later)
    if visible is not None:
        a = jnp.where(visible, a, 0.0)
    return a.astype(BF16)


def _sb_attn_kernel(q_ref, k_ref, v_ref, o_ref, qm_ref, acc_ref, carry_ref):
    qi = pl.program_id(1)
    pairs = SB_HEADS // 2
    rows = SB_HEADS * Q_BLOCK
    upper_ones = _upper_ones(LANES)
    left = lax.broadcasted_iota(jnp.int32, (Q_BLOCK, LANES), 1) < HEAD_DIM
    for p in range(pairs):
        q2 = q_ref[:, p * LANES:(p + 1) * LANES]
        zero = jnp.zeros_like(q2)
        qm_ref[p] = jnp.concatenate([jnp.where(left, q2, zero), jnp.where(left, zero, q2)], axis=0)

    groups = ((0, 1), (2, 3))
    grows = rows // len(groups)

    def block(kb, first):
        keys = pl.ds(pl.multiple_of(kb * Q_BLOCK, Q_BLOCK), Q_BLOCK)
        zs = [jnp.concatenate([_dot_nt(qm_ref[p], k_ref[keys, p * LANES:(p + 1) * LANES]) for p in g], axis=0)
              for g in groups]
        visible = None
        if first:
            r = lax.broadcasted_iota(jnp.int32, (grows, LANES), 0) % Q_BLOCK
            visible = lax.broadcasted_iota(jnp.int32, (grows, LANES), 1) < r
        terms = [_sb_log_terms(z, upper_ones, visible) for z in zs]
        for gi, g in enumerate(groups):
            log_beta, later, total = terms[gi]
            grp = slice(gi * grows, (gi + 1) * grows)
            if first:
                carry_ref[grp] = total
            else:
                later = later + carry_ref[grp]
                carry_ref[grp] += total
            a = _sb_weights(log_beta, later, visible)
            for i, p in enumerate(g):
                pv = _dot(a[i * 2 * Q_BLOCK:(i + 1) * 2 * Q_BLOCK], v_ref[keys, p * LANES:(p + 1) * LANES])
                acc_ref[p] = pv if first else acc_ref[p] + pv

    block(qi, True)

    @pl.loop(0, qi)
    def _(j):
        block(qi - 1 - j, False)

    for p in range(pairs):
        o_ref[:, p * LANES:(p + 1) * LANES] = jnp.where(left, acc_ref[p, :Q_BLOCK], acc_ref[p, Q_BLOCK:])


def _sb_attn(q, k16, v16, batch, seq):
    nq = seq // Q_BLOCK
    return pl.pallas_call(
        _sb_attn_kernel, grid=(batch, nq),
        in_specs=[pl.BlockSpec((Q_BLOCK, SB_WIDTH), lambda b, i: (b * nq + i, 0)),
                  pl.BlockSpec((seq, SB_WIDTH), lambda b, i: (b, 0)),
                  pl.BlockSpec((seq, SB_WIDTH), lambda b, i: (b, 0))],
        out_specs=pl.BlockSpec((Q_BLOCK, SB_WIDTH), lambda b, i: (b * nq + i, 0)),
        out_shape=jax.ShapeDtypeStruct((batch * seq, SB_WIDTH), F32),
        scratch_shapes=[pltpu.VMEM((SB_HEADS // 2, 2 * Q_BLOCK, LANES), BF16),
                        pltpu.VMEM((SB_HEADS // 2, 2 * Q_BLOCK, LANES), F32),
                        pltpu.VMEM((SB_HEADS * Q_BLOCK, LANES), F32)],
        compiler_params=_params(("parallel", "parallel")), name="sb_attn",
    )(q, k16, v16)


def _mla_attn_kernel(q_ref, kv_ref, latt_ref, o_ref, m_ref, l_ref, acc_ref):
    qi = pl.program_id(1)
    cols = MLA_HEADS * Q_BLOCK
    q = q_ref[0].reshape(cols, QCAT)
    key = lax.broadcasted_iota(jnp.int32, (Q_BLOCK, cols), 0)
    qry = lax.broadcasted_iota(jnp.int32, (Q_BLOCK, cols), 1) % Q_BLOCK

    def scores(start, size):
        kv = kv_ref[pl.ds(pl.multiple_of(start, Q_BLOCK), size), :]
        return _dot_nt(kv, q) * (MLA_SCALE * LOG2E)

    s = jnp.where(key <= qry, scores(qi * Q_BLOCK, Q_BLOCK), NEG_INF)
    m = jnp.max(s, axis=0, keepdims=True)
    p = jnp.exp2(s - m)
    m_ref[...] = m
    l_ref[...] = jnp.sum(p, axis=0, keepdims=True)
    acc_ref[...] = _dot(latt_ref[qi], p.astype(BF16))

    def update(s, latt):
        m_prev = m_ref[...]
        m_new = jnp.maximum(m_prev, jnp.max(s, axis=0, keepdims=True))
        alpha = jnp.exp2(m_prev - m_new)
        p = jnp.exp2(s - m_new)
        l_ref[...] = alpha * l_ref[...] + jnp.sum(p, axis=0, keepdims=True)
        acc_ref[...] = alpha * acc_ref[...] + _dot(latt, p.astype(BF16))
        m_ref[...] = m_new

    @pl.when(qi % 2 == 1)
    def _():
        update(scores((qi - 1) * Q_BLOCK, Q_BLOCK), latt_ref[qi - 1])

    @pl.loop(0, qi // 2)
    def _(j):
        latt = jnp.concatenate([latt_ref[2 * j], latt_ref[2 * j + 1]], axis=1)
        update(scores(j * 2 * Q_BLOCK, 2 * Q_BLOCK), latt)

    out = acc_ref[...] / l_ref[...]
    for hd in range(MLA_HEADS):
        blk = slice(hd * Q_BLOCK, (hd + 1) * Q_BLOCK)
        o_ref[:, hd * KV_RANK:(hd + 1) * KV_RANK] = out[:, blk].T.astype(BF16)


def _mla_attn(qcat, kvcat, latt, batch, seq):
    nq = seq // Q_BLOCK
    cols = MLA_HEADS * Q_BLOCK
    return pl.pallas_call(
        _mla_attn_kernel, grid=(batch, nq),
        in_specs=[pl.BlockSpec((1, MLA_HEADS, Q_BLOCK, QCAT), lambda b, i: (b * nq + i, 0, 0, 0)),
                  pl.BlockSpec((seq, QCAT), lambda b, i: (b, 0)),
                  pl.BlockSpec((nq, KV_RANK, Q_BLOCK), lambda b, i: (b, 0, 0))],
        out_specs=pl.BlockSpec((Q_BLOCK, MLA_HEADS * KV_RANK), lambda b, i: (b * nq + i, 0)),
        out_shape=jax.ShapeDtypeStruct((batch * seq, MLA_HEADS * KV_RANK), BF16),
        scratch_shapes=[pltpu.VMEM((1, cols), F32), pltpu.VMEM((1, cols), F32), pltpu.VMEM((KV_RANK, cols), F32)],
        compiler_params=_params(("parallel", "parallel")), name="mla_attn",
    )(qcat, kvcat, latt)


def _decode_kernel(pt_ref, qsb_ref, knew_ref, vnew_ref, qcat_ref, latnew_ref, krnew_ref, *refs):
    n = PAGES_PER_STEP
    kt_refs, vt_refs, lat_refs, krt_refs = refs[:n], refs[n:2 * n], refs[2 * n:3 * n], refs[3 * n:4 * n]
    osb_ref, omla_ref = refs[4 * n:4 * n + 2]
    acc_sb, carry_ref, m_ref, l_ref, acc_mla = refs[4 * n + 2:]
    step = pl.program_id(1)
    nq = qsb_ref.shape[0]
    rows = SB_HEADS * nq

    q_tiled = jnp.concatenate([qsb_ref[...]] * SB_HEADS, axis=0)
    row_head = lax.broadcasted_iota(jnp.int32, (rows, SB_WIDTH), 0) // nq
    lane_head = lax.broadcasted_iota(jnp.int32, (rows, SB_WIDTH), 1) // HEAD_DIM
    own_head = row_head == lane_head
    q_bd = jnp.where(own_head, q_tiled, 0.0).astype(BF16)
    qcat = qcat_ref[...].astype(BF16)
    q_lat = qcat[:, :KV_RANK]
    q_rope = qcat[:, KV_RANK:KV_RANK + MLA_ROPE]
    upper_ones = _upper_ones(PAGE)
    scale = MLA_SCALE * LOG2E

    @pl.when(step == 0)
    def _():
        pad = PAGE - nq
        q_idx = lax.broadcasted_iota(jnp.int32, (rows, PAGE), 0) % nq
        key_idx = lax.broadcasted_iota(jnp.int32, (rows, PAGE), 1)
        knew = jnp.concatenate([knew_ref[...], jnp.zeros((pad, SB_WIDTH), F32)], axis=0).astype(BF16)
        vnew = jnp.concatenate([vnew_ref[...], jnp.zeros((pad, SB_WIDTH), F32)], axis=0).astype(BF16)
        visible = key_idx < q_idx
        log_beta, later, total = _sb_log_terms(_dot_nt(q_bd, knew), upper_ones, visible)
        carry_ref[...] = total
        acc_sb[...] = _dot(_sb_weights(log_beta, later, visible), vnew)
        lat = jnp.concatenate([latnew_ref[...], jnp.zeros((pad, KV_RANK), F32)], axis=0).astype(BF16)
        kr = jnp.concatenate([krnew_ref[...], jnp.zeros((pad, MLA_ROPE), F32)], axis=0).astype(BF16)
        s = jnp.where(key_idx <= q_idx, (_dot_nt(q_lat, lat) + _dot_nt(q_rope, kr)) * scale, NEG_INF)
        m = jnp.max(s, axis=1, keepdims=True)
        p = jnp.exp2(s - m)
        m_ref[...] = m
        l_ref[...] = jnp.sum(p, axis=1, keepdims=True)
        acc_mla[...] = _dot(p.astype(BF16), lat)

    z = jnp.concatenate([_dot(q_bd, kt_refs[j][...].astype(BF16)) for j in range(n)], axis=0)
    lats = [lat_refs[j][...].astype(BF16) for j in range(n)]
    s = jnp.concatenate([_dot_nt(q_lat, lats[j]) + _dot(q_rope, krt_refs[j][...].astype(BF16)) for j in range(n)],
                        axis=1) * scale
    log_beta, later, total = _sb_log_terms(z, upper_ones, None)
    carry = carry_ref[...]
    acc = acc_sb[...]
    for j in range(n):
        page = slice(j * rows, (j + 1) * rows)
        a = _sb_weights(log_beta[page], later[page] + carry, None)
        carry = carry + total[page]
        acc = acc + _dot_nt(a, vt_refs[j][...].astype(BF16))
    carry_ref[...] = carry
    acc_sb[...] = acc

    m = m_ref[...]
    m_new = jnp.maximum(m, jnp.max(s, axis=1, keepdims=True))
    alpha = jnp.exp2(m - m_new)
    p = jnp.exp2(s - m_new)
    l = alpha * l_ref[...] + jnp.sum(p, axis=1, keepdims=True)
    p = p.astype(BF16)
    accm = alpha * acc_mla[...]
    for j in range(n):
        accm = accm + _dot(p[:, j * PAGE:(j + 1) * PAGE], lats[j])
    m_ref[...] = m_new
    l_ref[...] = l
    acc_mla[...] = accm

    @pl.when(step == pl.num_programs(1) - 1)
    def _():
        own = jnp.where(own_head, acc, 0.0)
        out = own[:nq]
        for hd in range(1, SB_HEADS):
            out = out + own[hd * nq:(hd + 1) * nq]
        osb_ref[...] = out
        omla_ref[...] = accm / l


def _decode(layer, page_table, qsb, knew, vnew, qcat, latnew, krnew, ckt, cvt, clat, ckrt):
    nb, nq, _ = qsb.shape
    n_pages = page_table.shape[1]
    n = PAGES_PER_STEP
    steps = n_pages // n
    rows = SB_HEADS * nq

    def per_seq(shape):
        return pl.BlockSpec((None,) + shape, lambda b, s, pt: (b, 0, 0))

    def page_spec(shape, j):
        return pl.BlockSpec((None, None) + shape,
                            lambda b, s, pt, j=j: (layer, pt[b, n_pages - 1 - (s * n + j)], 0, 0))

    in_specs = [per_seq((nq, SB_WIDTH)), per_seq((nq, SB_WIDTH)), per_seq((nq, SB_WIDTH)), per_seq((rows, QCAT)),
                per_seq((nq, KV_RANK)), per_seq((nq, MLA_ROPE))]
    for shape in ((SB_WIDTH, PAGE), (SB_WIDTH, PAGE), (PAGE, KV_RANK), (MLA_ROPE, PAGE)):
        in_specs += [page_spec(shape, j) for j in range(n)]
    grid_spec = pltpu.PrefetchScalarGridSpec(
        num_scalar_prefetch=1, grid=(nb, steps), in_specs=in_specs,
        out_specs=(per_seq((nq, SB_WIDTH)), per_seq((rows, KV_RANK))),
        scratch_shapes=[pltpu.VMEM((rows, SB_WIDTH), F32), pltpu.VMEM((rows, PAGE), F32), pltpu.VMEM((rows, 1), F32),
                        pltpu.VMEM((rows, 1), F32), pltpu.VMEM((rows, KV_RANK), F32)])
    return pl.pallas_call(
        _decode_kernel, grid_spec=grid_spec,
        out_shape=(jax.ShapeDtypeStruct((nb, nq, SB_WIDTH), F32), jax.ShapeDtypeStruct((nb, rows, KV_RANK), F32)),
        compiler_params=_params(("parallel", "arbitrary")), name="decode_attn",
    )(page_table, qsb, knew, vnew, qcat, latnew, krnew, *([ckt] * n), *([cvt] * n), *([clat] * n), *([ckrt] * n))


def _post_kernel(osb_ref, omla_ref, h_ref, gsb_ref, gmla_ref, wuv_ref, wout_ref, gffn_ref, h1_ref, hn_ref):
    pairs = [_dot(omla_ref[:, p * 2 * KV_RANK:(p + 1) * 2 * KV_RANK], wuv_ref[p]) for p in range(MLA_HEADS // 2)]
    o_mla = jnp.concatenate(pairs, axis=1)
    mixed = jnp.concatenate([_rms(osb_ref[...], gsb_ref[...]), _rms(o_mla, gmla_ref[...])], axis=1).astype(BF16)
    h1 = h_ref[...] + _dot(mixed, wout_ref[...])
    h1_ref[...] = h1
    hn_ref[...] = _rms(h1, gffn_ref[...]).astype(BF16)


def _post(osb, omla, h, w, tm):
    t = h.shape[0]
    row = lambda n: pl.BlockSpec((tm, n), lambda i: (i, 0))
    return pl.pallas_call(
        _post_kernel, grid=(t // tm,),
        in_specs=[row(SB_WIDTH), row(MLA_HEADS * KV_RANK), row(D_MODEL), _full((1, SB_WIDTH)), _full((1, MLA_WIDTH)),
                  _full((MLA_HEADS // 2, 2 * KV_RANK, 2 * MLA_V)), _full((D_MODEL, D_MODEL)), _full((1, D_MODEL))],
        out_specs=(row(D_MODEL), row(D_MODEL)),
        out_shape=(jax.ShapeDtypeStruct((t, D_MODEL), F32), jax.ShapeDtypeStruct((t, D_MODEL), BF16)),
        compiler_params=_params(("parallel",)), name="post_attn",
    )(osb, omla, h, w["g_out_sb"], w["g_out_mla"], w["w_uv_bd"], w["w_out"], w["g_ffn"])


def _swiglu_chunk(x, wg, wu, wd):
    g = _dot(x, wg)
    act = (g * jax.nn.sigmoid(g) * _dot(x, wu)).astype(BF16)
    return _dot(act, wd)


def _ffn_kernel(hn_ref, h1_ref, wg_ref, wu_ref, wd_ref, gfin_ref, o_ref, acc_ref, *, final_norm):
    j = pl.program_id(1)

    @pl.when(j == 0)
    def _():
        acc_ref[...] = jnp.zeros_like(acc_ref)

    acc_ref[...] += _swiglu_chunk(hn_ref[...], wg_ref[...], wu_ref[...], wd_ref[...])

    @pl.when(j == pl.num_programs(1) - 1)
    def _():
        out = h1_ref[...] + acc_ref[...]
        o_ref[...] = _rms(out, gfin_ref[...]) if final_norm else out


def _ffn(hn, h1, wg, wu, wd, g_final, tm, tf, final_norm):
    t = hn.shape[0]
    f = wg.shape[1]
    return pl.pallas_call(
        functools.partial(_ffn_kernel, final_norm=final_norm), grid=(t // tm, f // tf),
        in_specs=[pl.BlockSpec((tm, D_MODEL), lambda i, j: (i, 0)), pl.BlockSpec((tm, D_MODEL), lambda i, j: (i, 0)),
                  pl.BlockSpec((D_MODEL, tf), lambda i, j: (0, j)), pl.BlockSpec((D_MODEL, tf), lambda i, j: (0, j)),
                  pl.BlockSpec((tf, D_MODEL), lambda i, j: (j, 0)), _full((1, D_MODEL))],
        out_specs=pl.BlockSpec((tm, D_MODEL), lambda i, j: (i, 0)),
        out_shape=jax.ShapeDtypeStruct((t, D_MODEL), F32),
        scratch_shapes=[pltpu.VMEM((tm, D_MODEL), F32)],
        compiler_params=_params(("parallel", "arbitrary")), name="ffn_dense",
    )(hn, h1, wg, wu, wd, g_final)


def _router_kernel(h1_ref, gffn_ref, wr_ref, comb_ref):
    hn = _rms(h1_ref[...], gffn_ref[...])
    logits = jnp.dot(hn, wr_ref[...], precision=lax.Precision.HIGHEST, preferred_element_type=F32)
    lane = lax.broadcasted_iota(jnp.int32, logits.shape, 1)
    logits = jnp.where(lane < N_EXPERTS, logits, NEG_INF)
    m1 = jnp.max(logits, axis=1, keepdims=True)
    i1 = jnp.min(jnp.where(logits == m1, lane, LANES), axis=1, keepdims=True)
    rest = jnp.where(lane == i1, NEG_INF, logits)
    m2 = jnp.max(rest, axis=1, keepdims=True)
    i2 = jnp.min(jnp.where(rest == m2, lane, LANES), axis=1, keepdims=True)
    e2 = jnp.exp(m2 - m1)
    g1 = 1.0 / (1.0 + e2)
    g2 = e2 / (1.0 + e2)
    comb_ref[...] = jnp.where(lane == i1, g1, 0.0) + jnp.where(lane == i2, g2, 0.0)


def _router(h1, g_ffn, w_router, tm):
    t = h1.shape[0]
    return pl.pallas_call(
        _router_kernel, grid=(t // tm,),
        in_specs=[pl.BlockSpec((tm, D_MODEL), lambda i: (i, 0)), _full((1, D_MODEL)), _full((D_MODEL, LANES))],
        out_specs=pl.BlockSpec((tm, LANES), lambda i: (i, 0)),
        out_shape=jax.ShapeDtypeStruct((t, LANES), F32),
        compiler_params=_params(("parallel",)), name="router",
    )(h1, g_ffn, w_router)


def _moe_kernel(hn_ref, h1_ref, comb_ref, wg_ref, wu_ref, wd_ref, gfin_ref, o_ref, acc_ref, *, final_norm):
    e = pl.program_id(1)
    j = pl.program_id(2)

    @pl.when(jnp.logical_and(e == 0, j == 0))
    def _():
        acc_ref[...] = jnp.zeros_like(acc_ref)

    lane = lax.broadcasted_iota(jnp.int32, comb_ref.shape, 1)
    gate = jnp.sum(jnp.where(lane == e, comb_ref[...], 0.0), axis=1, keepdims=True)
    acc_ref[...] += gate * _swiglu_chunk(hn_ref[...], wg_ref[...], wu_ref[...], wd_ref[...])

    @pl.when(jnp.logical_and(e == pl.num_programs(1) - 1, j == pl.num_programs(2) - 1))
    def _():
        out = h1_ref[...] + acc_ref[...]
        o_ref[...] = _rms(out, gfin_ref[...]) if final_norm else out


def _moe(hn, h1, comb, wg, wu, wd, g_final, tm, tf, final_norm):
    t = hn.shape[0]
    ne, _, f = wg.shape
    tok = lambda n: pl.BlockSpec((tm, n), lambda i, e, j: (i, 0))
    return pl.pallas_call(
        functools.partial(_moe_kernel, final_norm=final_norm), grid=(t // tm, ne, f // tf),
        in_specs=[tok(D_MODEL), tok(D_MODEL), tok(LANES),
                  pl.BlockSpec((None, D_MODEL, tf), lambda i, e, j: (e, 0, j)),
                  pl.BlockSpec((None, D_MODEL, tf), lambda i, e, j: (e, 0, j)),
                  pl.BlockSpec((None, tf, D_MODEL), lambda i, e, j: (e, j, 0)), _full((1, D_MODEL))],
        out_specs=tok(D_MODEL),
        out_shape=jax.ShapeDtypeStruct((t, D_MODEL), F32),
        scratch_shapes=[pltpu.VMEM((tm, D_MODEL), F32)],
        compiler_params=_params(("parallel", "arbitrary", "arbitrary")), name="ffn_moe",
    )(hn, h1, comb, wg, wu, wd, g_final)


def _prep_layer(l, g_attn_norm, w_in, g_q_norm, w_uq, g_kv_norm, w_uk, w_uv, g_out_sb, g_out_mla, w_out, g_ffn_norm):
    half = MLA_ROPE // 2
    o = 3 * SB_WIDTH + Q_RANK + KV_RANK
    kr = w_in[l][:, o:]
    pad = jnp.zeros((D_MODEL, LANES - MLA_ROPE), F32)
    kr_a = jnp.concatenate([kr, pad], axis=1)
    kr_b = jnp.concatenate([-kr[:, half:], kr[:, :half], pad], axis=1)
    w_in_p = jnp.concatenate([w_in[l][:, :o], kr_a, kr_b], axis=1).astype(BF16)

    uq = w_uq[l]
    w_nope = uq[:, :, :MLA_NOPE].reshape(Q_RANK, MLA_HEADS * MLA_NOPE).astype(BF16)
    qr = uq[:, :, MLA_NOPE:]
    hpad = jnp.zeros((Q_RANK, MLA_HEADS, LANES - MLA_ROPE), F32)
    w_rope_a = jnp.concatenate([qr, hpad], axis=2).reshape(Q_RANK, MLA_HEADS * LANES).astype(BF16)
    w_rope_b = jnp.concatenate([-qr[:, :, half:], qr[:, :, :half], hpad], axis=2)
    w_rope_b = w_rope_b.reshape(Q_RANK, MLA_HEADS * LANES).astype(BF16)

    uk = jnp.transpose(w_uk[l], (1, 2, 0)).reshape(MLA_HEADS // 2, 2, MLA_NOPE, KV_RANK)
    zk = jnp.zeros((MLA_HEADS // 2, MLA_NOPE, KV_RANK), F32)
    w_uk_bd = jnp.concatenate([jnp.concatenate([uk[:, 0], zk], axis=2),
                               jnp.concatenate([zk, uk[:, 1]], axis=2)], axis=1).astype(BF16)
    uv = jnp.transpose(w_uv[l], (1, 0, 2)).reshape(MLA_HEADS // 2, 2, KV_RANK, MLA_V)
    zv = jnp.zeros((MLA_HEADS // 2, KV_RANK, MLA_V), F32)
    w_uv_bd = jnp.concatenate([jnp.concatenate([uv[:, 0], zv], axis=2),
                               jnp.concatenate([zv, uv[:, 1]], axis=2)], axis=1).astype(BF16)
    return {
        "g_attn": g_attn_norm[l][None], "w_in": w_in_p, "g_q": g_q_norm[l][None], "w_nope": w_nope,
        "w_rope_a": w_rope_a, "w_rope_b": w_rope_b, "g_kv": g_kv_norm[l][None], "w_uk_bd": w_uk_bd,
        "w_uv_bd": w_uv_bd, "g_out_sb": g_out_sb[l][None], "g_out_mla": g_out_mla[l][None],
        "w_out": w_out[l].astype(BF16), "g_ffn": g_ffn_norm[l][None],
    }


def _rope_tables(pos):
    inv_freq = 1.0 / (ROPE_THETA ** (jnp.arange(0, MLA_ROPE, 2, dtype=F32) / MLA_ROPE))
    ang = pos.astype(F32)[:, None] * inv_freq[None, :]
    pad = jnp.zeros((pos.shape[0], LANES - MLA_ROPE), F32)
    cos, sin = jnp.cos(ang), jnp.sin(ang)
    return jnp.concatenate([cos, cos, pad], axis=1), jnp.concatenate([sin, sin, pad], axis=1)


def kernel(x_prompt, x_sample, cache_sb_k, cache_sb_v, cache_mla_latent, cache_mla_krope, page_table, g_attn_norm,
           w_in, g_q_norm, w_uq, g_kv_norm, w_uk, w_uv, g_out_sb, g_out_mla, w_out, g_ffn_norm, w_gate_dense,
           w_up_dense, w_down_dense, w_router, w_gate_moe, w_up_moe, w_down_moe, g_final):
    batch, seq, _ = x_prompt.shape
    nb, nq, _ = x_sample.shape
    depth, n_pool = cache_sb_k.shape[:2]
    past_len = page_table.shape[1] * PAGE
    tp, ts = batch * seq, nb * nq

    ckt = jnp.transpose(cache_sb_k, (0, 1, 3, 4, 2)).reshape(depth, n_pool, SB_WIDTH, PAGE)
    cvt = jnp.transpose(cache_sb_v, (0, 1, 3, 4, 2)).reshape(depth, n_pool, SB_WIDTH, PAGE)
    ckrt = jnp.transpose(cache_mla_krope, (0, 1, 3, 2))
    cos_p, sin_p = _rope_tables(jnp.tile(jnp.arange(seq, dtype=jnp.int32), batch))
    cos_s, sin_s = _rope_tables(jnp.tile(past_len + jnp.arange(nq, dtype=jnp.int32), nb))
    g_fin = g_final[None]

    h_p = x_prompt.reshape(tp, D_MODEL)
    h_s = x_sample.reshape(ts, D_MODEL)
    rows_p, rows_s = [], []
    for l in range(depth):
        w = _prep_layer(l, g_attn_norm, w_in, g_q_norm, w_uq, g_kv_norm, w_uk, w_uv, g_out_sb, g_out_mla, w_out,
                        g_ffn_norm)
        last = l == depth - 1

        q, k, v, k16, v16, lat, kr, kvcat, latt, qcat = _proj(h_p, w["g_attn"], w, cos_p, sin_p, 512)
        rows_p.append((k, v, lat, kr))
        o_sb = _sb_attn(q, k16, v16, batch, seq)
        o_mla = _mla_attn(qcat, kvcat, latt, batch, seq)
        h1_p, hn_p = _post(o_sb, o_mla, h_p, w, 512)

        q, k, v, _, _, lat, kr, _, _, qcat = _proj(h_s, w["g_attn"], w, cos_s, sin_s, ts)
        rows_s.append((k, v, lat, kr))
        qcat_s = qcat.reshape(ts // Q_BLOCK, MLA_HEADS, Q_BLOCK // nq, nq, QCAT)
        qcat_s = jnp.transpose(qcat_s, (0, 2, 1, 3, 4)).reshape(nb, MLA_HEADS * nq, QCAT).astype(F32)
        o_sb, o_mla = _decode(l, page_table, q.astype(F32).reshape(nb, nq, SB_WIDTH), k.reshape(nb, nq, SB_WIDTH),
                              v.reshape(nb, nq, SB_WIDTH), qcat_s, lat.reshape(nb, nq, KV_RANK),
                              kr.reshape(nb, nq, MLA_ROPE), ckt, cvt, cache_mla_latent, ckrt)
        o_mla = jnp.transpose(o_mla.reshape(nb, MLA_HEADS, nq, KV_RANK), (0, 2, 1, 3))
        o_mla = o_mla.reshape(ts, MLA_HEADS * KV_RANK).astype(BF16)
        h1_s, hn_s = _post(o_sb.reshape(ts, SB_WIDTH), o_mla, h_s, w, ts)

        i = l // 2
        if l % 2 == 0:
            wg, wu, wd = w_gate_dense[i].astype(BF16), w_up_dense[i].astype(BF16), w_down_dense[i].astype(BF16)
            h_p = _ffn(hn_p, h1_p, wg, wu, wd, g_fin, 1024, 512, last)
            h_s = _ffn(hn_s, h1_s, wg, wu, wd, g_fin, ts, 512, last)
        else:
            wg, wu, wd = w_gate_moe[i].astype(BF16), w_up_moe[i].astype(BF16), w_down_moe[i].astype(BF16)
            wr = jnp.concatenate([w_router[i], jnp.zeros((D_MODEL, LANES - N_EXPERTS), F32)], axis=1)
            h_p = _moe(hn_p, h1_p, _router(h1_p, w["g_ffn"], wr, 1024), wg, wu, wd, g_fin, 1024, 512, last)
            h_s = _moe(hn_s, h1_s, _router(h1_s, w["g_ffn"], wr, ts), wg, wu, wd, g_fin, ts, 512, last)

    y_prompt = h_p.reshape(batch, seq, D_MODEL)
    y_sample = h_s.reshape(nb, nq, D_MODEL)

    def stack(rows, idx, shape):
        return jnp.stack([r[idx].reshape(shape) for r in rows])

    return (y_prompt, y_sample,
            stack(rows_p, 0, (batch, seq, SB_HEADS, HEAD_DIM)), stack(rows_p, 1, (batch, seq, SB_HEADS, HEAD_DIM)),
            stack(rows_p, 2, (batch, seq, KV_RANK)), stack(rows_p, 3, (batch, seq, MLA_ROPE)),
            stack(rows_s, 0, (nb, nq, SB_HEADS, HEAD_DIM)), stack(rows_s, 1, (nb, nq, SB_HEADS, HEAD_DIM)),
            stack(rows_s, 2, (nb, nq, KV_RANK)), stack(rows_s, 3, (nb, nq, MLA_ROPE)))
```

```python
import functools

import jax
import jax.numpy as jnp
from jax import lax
from jax.experimental import pallas as pl
from jax.experimental.pallas import tpu as pltpu

F32 = jnp.float32
BF16 = jnp.bfloat16

D_MODEL = 1024
HEAD_DIM = 64
SB_HEADS = 8
SB_WIDTH = 512
MLA_HEADS = 8
MLA_NOPE = 64
MLA_ROPE = 32
MLA_V = 64
MLA_WIDTH = 512
Q_RANK = 256
KV_RANK = 128
ROPE_THETA = 10000.0
N_EXPERTS = 8
PAGE = 128
Q_BLOCK = 128
EPS = 1e-6
SB_SCALE = HEAD_DIM ** -0.5
MLA_SCALE = (MLA_NOPE + MLA_ROPE) ** -0.5
LOG2E = 1.4426950408889634

LANES = 128
QCAT = 2 * LANES
IN_COLS_PADDED = 3 * SB_WIDTH + Q_RANK + KV_RANK + 2 * LANES
VMEM_LIMIT = 52 * 1024 * 1024
PAGES_PER_STEP = 16
MOE_CHUNKS = 7
MOE_ROWS = 64 * MOE_CHUNKS
NEG_INF = float("-inf")

_NT = (((1,), (1,)), ((), ()))


def _dot(a, b):
    return jnp.dot(a, b, preferred_element_type=F32)


def _dot_nt(a, b):
    return lax.dot_general(a, b, _NT, preferred_element_type=F32)


def _rms(x, g):
    return x * lax.rsqrt(jnp.mean(x * x, axis=-1, keepdims=True) + EPS) * g


def _params(sem):
    return pltpu.CompilerParams(dimension_semantics=sem, vmem_limit_bytes=VMEM_LIMIT)


def _full(shape):
    return pl.BlockSpec(shape, lambda *_: (0,) * len(shape))


def _proj_kernel(h_ref, g_ref, win_ref, gq_ref, wnope_ref, wra_ref, wrb_ref, gkv_ref, wuk_ref, cos_ref, sin_ref,
                 q_ref, k_ref, v_ref, k16_ref, v16_ref, lat_ref, kr_ref, kvcat_ref, latt_ref, qcat_ref):
    tm = h_ref.shape[0]
    xn = _rms(h_ref[...], g_ref[...]).astype(BF16)
    qkv = _dot(xn, win_ref[:, :3 * SB_WIDTH])
    q_ref[...] = (qkv[:, :SB_WIDTH] * SB_SCALE).astype(BF16)
    k = qkv[:, SB_WIDTH:2 * SB_WIDTH]
    v = qkv[:, 2 * SB_WIDTH:]
    k_ref[...] = k
    v_ref[...] = v
    k16_ref[...] = k.astype(BF16)
    v16_ref[...] = v.astype(BF16)

    rest = _dot(xn, win_ref[:, 3 * SB_WIDTH:])
    cos = cos_ref[...]
    sin = sin_ref[...]
    cqn = _rms(rest[:, :Q_RANK], gq_ref[...]).astype(BF16)
    lat = _rms(rest[:, Q_RANK:Q_RANK + KV_RANK], gkv_ref[...])
    o = Q_RANK + KV_RANK
    krope = rest[:, o:o + LANES] * cos + rest[:, o + LANES:] * sin
    lat_ref[...] = lat
    kr_ref[...] = krope[:, :MLA_ROPE]
    kvcat_ref[:, :LANES] = lat.astype(BF16)
    kvcat_ref[:, LANES:] = krope.astype(BF16)
    for i in range(tm // Q_BLOCK):
        latt_ref[i] = lat[i * Q_BLOCK:(i + 1) * Q_BLOCK].T.astype(BF16)

    qnope = _dot(cqn, wnope_ref[...]).astype(BF16)
    ra = _dot(cqn, wra_ref[...])
    rb = _dot(cqn, wrb_ref[...])
    for p in range(MLA_HEADS // 2):
        ql = _dot(qnope[:, p * LANES:(p + 1) * LANES], wuk_ref[p]).astype(BF16)
        for s in range(2):
            hd = 2 * p + s
            qr = (ra[:, hd * LANES:(hd + 1) * LANES] * cos + rb[:, hd * LANES:(hd + 1) * LANES] * sin).astype(BF16)
            for i in range(tm // Q_BLOCK):
                rows = slice(i * Q_BLOCK, (i + 1) * Q_BLOCK)
                qcat_ref[i, hd, :, :LANES] = ql[rows, s * LANES:(s + 1) * LANES]
                qcat_ref[i, hd, :, LANES:] = qr[rows]


def _proj(h, g, w, cos_t, sin_t, tm):
    t = h.shape[0]
    row = lambda n: pl.BlockSpec((tm, n), lambda i: (i, 0))
    out_shape = (
        jax.ShapeDtypeStruct((t, SB_WIDTH), BF16),
        jax.ShapeDtypeStruct((t, SB_WIDTH), F32),
        jax.ShapeDtypeStruct((t, SB_WIDTH), F32),
        jax.ShapeDtypeStruct((t, SB_WIDTH), BF16),
        jax.ShapeDtypeStruct((t, SB_WIDTH), BF16),
        jax.ShapeDtypeStruct((t, KV_RANK), F32),
        jax.ShapeDtypeStruct((t, MLA_ROPE), F32),
        jax.ShapeDtypeStruct((t, QCAT), BF16),
        jax.ShapeDtypeStruct((t // Q_BLOCK, KV_RANK, Q_BLOCK), BF16),
        jax.ShapeDtypeStruct((t // Q_BLOCK, MLA_HEADS, Q_BLOCK, QCAT), BF16),
    )
    out_specs = (row(SB_WIDTH), row(SB_WIDTH), row(SB_WIDTH), row(SB_WIDTH), row(SB_WIDTH), row(KV_RANK),
                 row(MLA_ROPE), row(QCAT),
                 pl.BlockSpec((tm // Q_BLOCK, KV_RANK, Q_BLOCK), lambda i: (i, 0, 0)),
                 pl.BlockSpec((tm // Q_BLOCK, MLA_HEADS, Q_BLOCK, QCAT), lambda i: (i, 0, 0, 0)))
    in_specs = [row(D_MODEL), _full((1, D_MODEL)), _full((D_MODEL, IN_COLS_PADDED)), _full((1, Q_RANK)),
                _full((Q_RANK, MLA_HEADS * MLA_NOPE)), _full((Q_RANK, MLA_HEADS * LANES)),
                _full((Q_RANK, MLA_HEADS * LANES)), _full((1, KV_RANK)),
                _full((MLA_HEADS // 2, LANES, 2 * LANES)), row(LANES), row(LANES)]
    return pl.pallas_call(
        _proj_kernel, grid=(t // tm,), in_specs=in_specs, out_specs=out_specs, out_shape=out_shape,
        compiler_params=_params(("parallel",)), name="proj",
    )(h, g, w["w_in"], w["g_q"], w["w_nope"], w["w_rope_a"], w["w_rope_b"], w["g_kv"], w["w_uk_bd"], cos_t, sin_t)


def _upper_ones(n):
    r = lax.broadcasted_iota(jnp.int32, (2 * n, 2 * n), 0) % n
    c = lax.broadcasted_iota(jnp.int32, (2 * n, 2 * n), 1)
    return jnp.where(jnp.logical_or(r > c, c >= n), 1.0, 0.0).astype(BF16)


def _sb_log_terms(z, upper_ones, visible):
    n = z.shape[1]
    sp = jnp.log(1.0 + jnp.exp2(jnp.abs(z) * (-LOG2E)))
    log_beta = jnp.minimum(z, 0.0) - sp
    log_keep = log_beta - z
    if visible is not None:
        log_keep = jnp.where(visible, log_keep, 0.0)
    hi = log_keep.astype(BF16)
    lo = (log_keep - hi.astype(F32)).astype(BF16)
    sums = _dot(jnp.concatenate([hi, lo], axis=1), upper_ones)
    return log_beta, sums[:, :n], sums[:, n:]


def _sb_weights(log_beta, later, visible):
    a = jnp.exp(log_beta + later)
    if visible is not None:
        a = jnp.where(visible, a, 0.0)
    return a.astype(BF16)


def _sb_attn_kernel(q_ref, k_ref, v_ref, o_ref, qm_ref, acc_ref, carry_ref):
    qi = pl.program_id(1)
    pairs = SB_HEADS // 2
    rows = SB_HEADS * Q_BLOCK
    upper_ones = _upper_ones(LANES)
    left = lax.broadcasted_iota(jnp.int32, (Q_BLOCK, LANES), 1) < HEAD_DIM
    for p in range(pairs):
        q2 = q_ref[:, p * LANES:(p + 1) * LANES]
        zero = jnp.zeros_like(q2)
        qm_ref[p] = jnp.concatenate([jnp.where(left, q2, zero), jnp.where(left, zero, q2)], axis=0)

    groups = ((0, 1), (2, 3))
    grows = rows // len(groups)

    def block(kb, first):
        keys = pl.ds(pl.multiple_of(kb * Q_BLOCK, Q_BLOCK), Q_BLOCK)
        zs = [jnp.concatenate([_dot_nt(qm_ref[p], k_ref[keys, p * LANES:(p + 1) * LANES]) for p in g], axis=0)
              for g in groups]
        visible = None
        if first:
            r = lax.broadcasted_iota(jnp.int32, (grows, LANES), 0) % Q_BLOCK
            visible = lax.broadcasted_iota(jnp.int32, (grows, LANES), 1) < r
        terms = [_sb_log_terms(z, upper_ones, visible) for z in zs]
        for gi, g in enumerate(groups):
            log_beta, later, total = terms[gi]
            grp = slice(gi * grows, (gi + 1) * grows)
            if first:
                carry_ref[grp] = total
            else:
                later = later + carry_ref[grp]
                carry_ref[grp] += total
            a = _sb_weights(log_beta, later, visible)
            for i, p in enumerate(g):
                pv = _dot(a[i * 2 * Q_BLOCK:(i + 1) * 2 * Q_BLOCK], v_ref[keys, p * LANES:(p + 1) * LANES])
                acc_ref[p] = pv if first else acc_ref[p] + pv

    block(qi, True)

    @pl.loop(0, qi)
    def _(j):
        block(qi - 1 - j, False)

    for p in range(pairs):
        o_ref[:, p * LANES:(p + 1) * LANES] = jnp.where(left, acc_ref[p, :Q_BLOCK], acc_ref[p, Q_BLOCK:])


def _sb_attn(q, k16, v16, batch, seq):
    nq = seq // Q_BLOCK
    return pl.pallas_call(
        _sb_attn_kernel, grid=(batch, nq),
        in_specs=[pl.BlockSpec((Q_BLOCK, SB_WIDTH), lambda b, i: (b * nq + i, 0)),
                  pl.BlockSpec((seq, SB_WIDTH), lambda b, i: (b, 0)),
                  pl.BlockSpec((seq, SB_WIDTH), lambda b, i: (b, 0))],
        out_specs=pl.BlockSpec((Q_BLOCK, SB_WIDTH), lambda b, i: (b * nq + i, 0)),
        out_shape=jax.ShapeDtypeStruct((batch * seq, SB_WIDTH), F32),
        scratch_shapes=[pltpu.VMEM((SB_HEADS // 2, 2 * Q_BLOCK, LANES), BF16),
                        pltpu.VMEM((SB_HEADS // 2, 2 * Q_BLOCK, LANES), F32),
                        pltpu.VMEM((SB_HEADS * Q_BLOCK, LANES), F32)],
        compiler_params=_params(("parallel", "parallel")), name="sb_attn",
    )(q, k16, v16)


def _mla_attn_kernel(q_ref, kv_ref, latt_ref, o_ref, m_ref, l_ref, acc_ref):
    qi = pl.program_id(1)
    cols = MLA_HEADS * Q_BLOCK
    q = q_ref[0].reshape(cols, QCAT)
    key = lax.broadcasted_iota(jnp.int32, (Q_BLOCK, cols), 0)
    qry = lax.broadcasted_iota(jnp.int32, (Q_BLOCK, cols), 1) % Q_BLOCK

    def scores(start, size):
        kv = kv_ref[pl.ds(pl.multiple_of(start, Q_BLOCK), size), :]
        return _dot_nt(kv, q) * (MLA_SCALE * LOG2E)

    s = jnp.where(key <= qry, scores(qi * Q_BLOCK, Q_BLOCK), NEG_INF)
    m = jnp.max(s, axis=0, keepdims=True)
    p = jnp.exp2(s - m)
    m_ref[...] = m
    l_ref[...] = jnp.sum(p, axis=0, keepdims=True)
    acc_ref[...] = _dot(latt_ref[qi], p.astype(BF16))

    def update(s, latt):
        m_prev = m_ref[...]
        m_new = jnp.maximum(m_prev, jnp.max(s, axis=0, keepdims=True))
        alpha = jnp.exp2(m_prev - m_new)
        p = jnp.exp2(s - m_new)
        l_ref[...] = alpha * l_ref[...] + jnp.sum(p, axis=0, keepdims=True)
        acc_ref[...] = alpha * acc_ref[...] + _dot(latt, p.astype(BF16))
        m_ref[...] = m_new

    @pl.when(qi % 2 == 1)
    def _():
        update(scores((qi - 1) * Q_BLOCK, Q_BLOCK), latt_ref[qi - 1])

    @pl.loop(0, qi // 2)
    def _(j):
        latt = jnp.concatenate([latt_ref[2 * j], latt_ref[2 * j + 1]], axis=1)
        update(scores(j * 2 * Q_BLOCK, 2 * Q_BLOCK), latt)

    out = acc_ref[...] / l_ref[...]
    for hd in range(MLA_HEADS):
        blk = slice(hd * Q_BLOCK, (hd + 1) * Q_BLOCK)
        o_ref[:, hd * KV_RANK:(hd + 1) * KV_RANK] = out[:, blk].T.astype(BF16)


def _mla_attn(qcat, kvcat, latt, batch, seq):
    nq = seq // Q_BLOCK
    cols = MLA_HEADS * Q_BLOCK
    return pl.pallas_call(
        _mla_attn_kernel, grid=(batch, nq),
        in_specs=[pl.BlockSpec((1, MLA_HEADS, Q_BLOCK, QCAT), lambda b, i: (b * nq + i, 0, 0, 0)),
                  pl.BlockSpec((seq, QCAT), lambda b, i: (b, 0)),
                  pl.BlockSpec((nq, KV_RANK, Q_BLOCK), lambda b, i: (b, 0, 0))],
        out_specs=pl.BlockSpec((Q_BLOCK, MLA_HEADS * KV_RANK), lambda b, i: (b * nq + i, 0)),
        out_shape=jax.ShapeDtypeStruct((batch * seq, MLA_HEADS * KV_RANK), BF16),
        scratch_shapes=[pltpu.VMEM((1, cols), F32), pltpu.VMEM((1, cols), F32), pltpu.VMEM((KV_RANK, cols), F32)],
        compiler_params=_params(("parallel", "parallel")), name="mla_attn",
    )(qcat, kvcat, latt)


def _decode_kernel(pt_ref, qsb_ref, knew_ref, vnew_ref, qcat_ref, latnew_ref, krnew_ref, *refs):
    n = PAGES_PER_STEP
    kt_refs, vt_refs, lat_refs, krt_refs = refs[:n], refs[n:2 * n], refs[2 * n:3 * n], refs[3 * n:4 * n]
    osb_ref, omla_ref = refs[4 * n:4 * n + 2]
    acc_sb, carry_ref, m_ref, l_ref, acc_mla = refs[4 * n + 2:]
    step = pl.program_id(1)
    nq = qsb_ref.shape[0]
    rows = SB_HEADS * nq

    q_tiled = jnp.concatenate([qsb_ref[...]] * SB_HEADS, axis=0)
    row_head = lax.broadcasted_iota(jnp.int32, (rows, SB_WIDTH), 0) // nq
    lane_head = lax.broadcasted_iota(jnp.int32, (rows, SB_WIDTH), 1) // HEAD_DIM
    own_head = row_head == lane_head
    q_bd = jnp.where(own_head, q_tiled, 0.0).astype(BF16)
    qcat = qcat_ref[...].astype(BF16)
    q_lat = qcat[:, :KV_RANK]
    q_rope = qcat[:, KV_RANK:KV_RANK + MLA_ROPE]
    upper_ones = _upper_ones(PAGE)
    scale = MLA_SCALE * LOG2E

    @pl.when(step == 0)
    def _():
        pad = PAGE - nq
        q_idx = lax.broadcasted_iota(jnp.int32, (rows, PAGE), 0) % nq
        key_idx = lax.broadcasted_iota(jnp.int32, (rows, PAGE), 1)
        knew = jnp.concatenate([knew_ref[...], jnp.zeros((pad, SB_WIDTH), F32)], axis=0).astype(BF16)
        vnew = jnp.concatenate([vnew_ref[...], jnp.zeros((pad, SB_WIDTH), F32)], axis=0).astype(BF16)
        visible = key_idx < q_idx
        log_beta, later, total = _sb_log_terms(_dot_nt(q_bd, knew), upper_ones, visible)
        carry_ref[...] = total
        acc_sb[...] = _dot(_sb_weights(log_beta, later, visible), vnew)
        lat = jnp.concatenate([latnew_ref[...], jnp.zeros((pad, KV_RANK), F32)], axis=0).astype(BF16)
        kr = jnp.concatenate([krnew_ref[...], jnp.zeros((pad, MLA_ROPE), F32)], axis=0).astype(BF16)
        s = jnp.where(key_idx <= q_idx, (_dot_nt(q_lat, lat) + _dot_nt(q_rope, kr)) * scale, NEG_INF)
        m = jnp.max(s, axis=1, keepdims=True)
        p = jnp.exp2(s - m)
        m_ref[...] = m
        l_ref[...] = jnp.sum(p, axis=1, keepdims=True)
        acc_mla[...] = _dot(p.astype(BF16), lat)

    z = jnp.concatenate([_dot(q_bd, kt_refs[j][...].astype(BF16)) for j in range(n)], axis=0)
    lats = [lat_refs[j][...].astype(BF16) for j in range(n)]
    s = jnp.concatenate([_dot_nt(q_lat, lats[j]) + _dot(q_rope, krt_refs[j][...].astype(BF16)) for j in range(n)],
                        axis=1) * scale
    log_beta, later, total = _sb_log_terms(z, upper_ones, None)
    carry = carry_ref[...]
    acc = acc_sb[...]
    for j in range(n):
        page = slice(j * rows, (j + 1) * rows)
        a = _sb_weights(log_beta[page], later[page] + carry, None)
        carry = carry + total[page]
        acc = acc + _dot_nt(a, vt_refs[j][...].astype(BF16))
    carry_ref[...] = carry
    acc_sb[...] = acc

    m = m_ref[...]
    m_new = jnp.maximum(m, jnp.max(s, axis=1, keepdims=True))
    alpha = jnp.exp2(m - m_new)
    p = jnp.exp2(s - m_new)
    l = alpha * l_ref[...] + jnp.sum(p, axis=1, keepdims=True)
    p = p.astype(BF16)
    accm = alpha * acc_mla[...]
    for j in range(n):
        accm = accm + _dot(p[:, j * PAGE:(j + 1) * PAGE], lats[j])
    m_ref[...] = m_new
    l_ref[...] = l
    acc_mla[...] = accm

    @pl.when(step == pl.num_programs(1) - 1)
    def _():
        own = jnp.where(own_head, acc, 0.0)
        out = own[:nq]
        for hd in range(1, SB_HEADS):
            out = out + own[hd * nq:(hd + 1) * nq]
        osb_ref[...] = out
        omla_ref[...] = accm / l


def _decode(layer, page_table, qsb, knew, vnew, qcat, latnew, krnew, ckt, cvt, clat, ckrt):
    nb, nq, _ = qsb.shape
    n_pages = page_table.shape[1]
    n = PAGES_PER_STEP
    steps = n_pages // n
    rows = SB_HEADS * nq

    def per_seq(shape):
        return pl.BlockSpec((None,) + shape, lambda b, s, pt: (b, 0, 0))

    def page_spec(shape, j):
        return pl.BlockSpec((None, None) + shape,
                            lambda b, s, pt, j=j: (layer, pt[b, n_pages - 1 - (s * n + j)], 0, 0))

    in_specs = [per_seq((nq, SB_WIDTH)), per_seq((nq, SB_WIDTH)), per_seq((nq, SB_WIDTH)), per_seq((rows, QCAT)),
                per_seq((nq, KV_RANK)), per_seq((nq, MLA_ROPE))]
    for shape in ((SB_WIDTH, PAGE), (SB_WIDTH, PAGE), (PAGE, KV_RANK), (MLA_ROPE, PAGE)):
        in_specs += [page_spec(shape, j) for j in range(n)]
    grid_spec = pltpu.PrefetchScalarGridSpec(
        num_scalar_prefetch=1, grid=(nb, steps), in_specs=in_specs,
        out_specs=(per_seq((nq, SB_WIDTH)), per_seq((rows, KV_RANK))),
        scratch_shapes=[pltpu.VMEM((rows, SB_WIDTH), F32), pltpu.VMEM((rows, PAGE), F32), pltpu.VMEM((rows, 1), F32),
                        pltpu.VMEM((rows, 1), F32), pltpu.VMEM((rows, KV_RANK), F32)])
    return pl.pallas_call(
        _decode_kernel, grid_spec=grid_spec,
        out_shape=(jax.ShapeDtypeStruct((nb, nq, SB_WIDTH), F32), jax.ShapeDtypeStruct((nb, rows, KV_RANK), F32)),
        compiler_params=_params(("parallel", "arbitrary")), name="decode_attn",
    )(page_table, qsb, knew, vnew, qcat, latnew, krnew, *([ckt] * n), *([cvt] * n), *([clat] * n), *([ckrt] * n))


def _post_kernel(osb_ref, omla_ref, h_ref, gsb_ref, gmla_ref, wuv_ref, wout_ref, gffn_ref, h1_ref, hn_ref):
    pairs = [_dot(omla_ref[:, p * 2 * KV_RANK:(p + 1) * 2 * KV_RANK], wuv_ref[p]) for p in range(MLA_HEADS // 2)]
    o_mla = jnp.concatenate(pairs, axis=1)
    mixed = jnp.concatenate([_rms(osb_ref[...], gsb_ref[...]), _rms(o_mla, gmla_ref[...])], axis=1).astype(BF16)
    h1 = h_ref[...] + _dot(mixed, wout_ref[...])
    h1_ref[...] = h1
    hn_ref[...] = _rms(h1, gffn_ref[...]).astype(BF16)


def _post(osb, omla, h, w, tm):
    t = h.shape[0]
    row = lambda n: pl.BlockSpec((tm, n), lambda i: (i, 0))
    return pl.pallas_call(
        _post_kernel, grid=(t // tm,),
        in_specs=[row(SB_WIDTH), row(MLA_HEADS * KV_RANK), row(D_MODEL), _full((1, SB_WIDTH)), _full((1, MLA_WIDTH)),
                  _full((MLA_HEADS // 2, 2 * KV_RANK, 2 * MLA_V)), _full((D_MODEL, D_MODEL)), _full((1, D_MODEL))],
        out_specs=(row(D_MODEL), row(D_MODEL)),
        out_shape=(jax.ShapeDtypeStruct((t, D_MODEL), F32), jax.ShapeDtypeStruct((t, D_MODEL), BF16)),
        compiler_params=_params(("parallel",)), name="post_attn",
    )(osb, omla, h, w["g_out_sb"], w["g_out_mla"], w["w_uv_bd"], w["w_out"], w["g_ffn"])


def _swiglu_chunk(x, wg, wu, wd):
    g = _dot(x, wg)
    act = (g * jax.nn.sigmoid(g) * _dot(x, wu)).astype(BF16)
    return _dot(act, wd)


def _ffn_kernel(hn_ref, h1_ref, wg_ref, wu_ref, wd_ref, gfin_ref, o_ref, acc_ref, *, final_norm):
    j = pl.program_id(1)

    @pl.when(j == 0)
    def _():
        acc_ref[...] = jnp.zeros_like(acc_ref)

    acc_ref[...] += _swiglu_chunk(hn_ref[...], wg_ref[...], wu_ref[...], wd_ref[...])

    @pl.when(j == pl.num_programs(1) - 1)
    def _():
        out = h1_ref[...] + acc_ref[...]
        o_ref[...] = _rms(out, gfin_ref[...]) if final_norm else out


def _ffn(hn, h1, wg, wu, wd, g_final, tm, tf, final_norm):
    t = hn.shape[0]
    f = wg.shape[1]
    return pl.pallas_call(
        functools.partial(_ffn_kernel, final_norm=final_norm), grid=(t // tm, f // tf),
        in_specs=[pl.BlockSpec((tm, D_MODEL), lambda i, j: (i, 0)), pl.BlockSpec((tm, D_MODEL), lambda i, j: (i, 0)),
                  pl.BlockSpec((D_MODEL, tf), lambda i, j: (0, j)), pl.BlockSpec((D_MODEL, tf), lambda i, j: (0, j)),
                  pl.BlockSpec((tf, D_MODEL), lambda i, j: (j, 0)), _full((1, D_MODEL))],
        out_specs=pl.BlockSpec((tm, D_MODEL), lambda i, j: (i, 0)),
        out_shape=jax.ShapeDtypeStruct((t, D_MODEL), F32),
        scratch_shapes=[pltpu.VMEM((tm, D_MODEL), F32)],
        compiler_params=_params(("parallel", "arbitrary")), name="ffn_dense",
    )(hn, h1, wg, wu, wd, g_final)


def _router_kernel(h1_ref, gffn_ref, wr_ref, comb_ref, meta_ref):
    hn = _rms(h1_ref[...], gffn_ref[...])
    logits = jnp.dot(hn, wr_ref[...], precision=lax.Precision.HIGHEST, preferred_element_type=F32)
    lane = lax.broadcasted_iota(jnp.int32, logits.shape, 1)
    logits = jnp.where(lane < N_EXPERTS, logits, NEG_INF)
    m1 = jnp.max(logits, axis=1, keepdims=True)
    i1 = jnp.min(jnp.where(logits == m1, lane, LANES), axis=1, keepdims=True)
    rest = jnp.where(lane == i1, NEG_INF, logits)
    m2 = jnp.max(rest, axis=1, keepdims=True)
    i2 = jnp.min(jnp.where(rest == m2, lane, LANES), axis=1, keepdims=True)
    e2 = jnp.exp(m2 - m1)
    g1 = 1.0 / (1.0 + e2)
    g2 = e2 / (1.0 + e2)
    comb_ref[...] = jnp.where(lane == i1, g1, 0.0) + jnp.where(lane == i2, g2, 0.0)
    meta_ref[...] = jnp.where(lane == 0, i1.astype(F32), jnp.where(lane == 1, i2.astype(F32),
                              jnp.where(lane == 2, g1, jnp.where(lane == 3, g2, 0.0))))


def _router(h1, g_ffn, w_router, tm):
    t = h1.shape[0]
    return pl.pallas_call(
        _router_kernel, grid=(t // tm,),
        in_specs=[pl.BlockSpec((tm, D_MODEL), lambda i: (i, 0)), _full((1, D_MODEL)), _full((D_MODEL, LANES))],
        out_specs=(pl.BlockSpec((tm, LANES), lambda i: (i, 0)), pl.BlockSpec((tm, LANES), lambda i: (i, 0))),
        out_shape=(jax.ShapeDtypeStruct((t, LANES), F32), jax.ShapeDtypeStruct((t, LANES), F32)),
        compiler_params=_params(("parallel",)), name="router",
    )(h1, g_ffn, w_router)


def _moe_kernel(hn_ref, h1_ref, comb_ref, wg_ref, wu_ref, wd_ref, gfin_ref, o_ref, acc_ref, *, final_norm):
    e = pl.program_id(1)
    j = pl.program_id(2)

    @pl.when(jnp.logical_and(e == 0, j == 0))
    def _():
        acc_ref[...] = jnp.zeros_like(acc_ref)

    lane = lax.broadcasted_iota(jnp.int32, comb_ref.shape, 1)
    gate = jnp.sum(jnp.where(lane == e, comb_ref[...], 0.0), axis=1, keepdims=True)
    acc_ref[...] += gate * _swiglu_chunk(hn_ref[...], wg_ref[...], wu_ref[...], wd_ref[...])

    @pl.when(jnp.logical_and(e == pl.num_programs(1) - 1, j == pl.num_programs(2) - 1))
    def _():
        out = h1_ref[...] + acc_ref[...]
        o_ref[...] = _rms(out, gfin_ref[...]) if final_norm else out


def _moe(hn, h1, comb, wg, wu, wd, g_final, tm, tf, final_norm):
    t = hn.shape[0]
    ne, _, f = wg.shape
    tok = lambda n: pl.BlockSpec((tm, n), lambda i, e, j: (i, 0))
    return pl.pallas_call(
        functools.partial(_moe_kernel, final_norm=final_norm), grid=(t // tm, ne, f // tf),
        in_specs=[tok(D_MODEL), tok(D_MODEL), tok(LANES),
                  pl.BlockSpec((None, D_MODEL, tf), lambda i, e, j: (e, 0, j)),
                  pl.BlockSpec((None, D_MODEL, tf), lambda i, e, j: (e, 0, j)),
                  pl.BlockSpec((None, tf, D_MODEL), lambda i, e, j: (e, j, 0)), _full((1, D_MODEL))],
        out_specs=tok(D_MODEL),
        out_shape=jax.ShapeDtypeStruct((t, D_MODEL), F32),
        scratch_shapes=[pltpu.VMEM((tm, D_MODEL), F32)],
        compiler_params=_params(("parallel", "arbitrary", "arbitrary")), name="ffn_moe",
    )(hn, h1, comb, wg, wu, wd, g_final)


def _routing_tables(e1, e2, n_tiles):
    t = e1.shape[0]
    tr = MOE_ROWS
    flat = jnp.stack([e1, e2], axis=1).reshape(2 * t)
    order = jnp.argsort(flat, stable=True).astype(jnp.int32)
    counts = jnp.sum(flat[:, None] == jnp.arange(N_EXPERTS, dtype=jnp.int32)[None, :], axis=0).astype(jnp.int32)
    padded = (counts + tr - 1) // tr * tr
    pend = jnp.cumsum(padded)
    pstart = pend - padded
    ustart = jnp.cumsum(counts) - counts
    n_valid = (pend[-1] // tr).astype(jnp.int32)
    tile_start = jnp.arange(n_tiles, dtype=jnp.int32) * tr
    tile_expert = jnp.minimum(jnp.sum(tile_start[:, None] >= pend[None, :], axis=1), N_EXPERTS - 1).astype(jnp.int32)
    slot = jnp.arange(n_tiles * tr, dtype=jnp.int32)
    e = jnp.repeat(tile_expert, tr)
    rank = slot - pstart[e]
    valid = jnp.logical_and(slot < pend[-1], rank < counts[e])
    a = order[jnp.clip(ustart[e] + rank, 0, 2 * t - 1)]
    src = jnp.where(valid, a // 2, 0)
    dst = jnp.where(valid, (a % 2) * t + a // 2, 2 * t + e * tr + jnp.clip(rank - counts[e], 0, tr - 1))
    dst = jnp.concatenate([2 * t + N_EXPERTS * tr + jnp.arange(tr, dtype=jnp.int32), dst])
    return src.reshape(n_tiles, 1, tr), dst.reshape(n_tiles + 1, 1, tr), tile_expert, n_valid.reshape(1)


def _moe_grouped_kernel(te_ref, nv_ref, src_cur, src_next, dst_prev, dst_cur, h1_hbm, gffn_ref, wg_ref, wu_ref, wd_ref,
                        y_hbm, xbuf, xn_ref, acc_ref, obuf, gsem, ssem):
    i = pl.program_id(0)
    j = pl.program_id(1)
    nj = pl.num_programs(1)
    nv = nv_ref[0]
    tr = xn_ref.shape[0]
    ch = tr // MOE_CHUNKS
    slot = i % 2
    other = 1 - slot

    def gather_row(tok_ref, r, s):
        return pltpu.make_async_copy(h1_hbm.at[pl.ds(tok_ref[0, r], 1)], xbuf.at[s, pl.ds(r, 1)], gsem.at[s])

    def scatter_row(row_ref, r, s):
        return pltpu.make_async_copy(obuf.at[s, pl.ds(r, 1)], y_hbm.at[pl.ds(row_ref[0, r], 1)], ssem.at[s])

    @pl.when(i < nv)
    def _():
        @pl.when(jnp.logical_and(i == 0, j == 0))
        def _():
            obuf[...] = jnp.zeros_like(obuf)
            for r in range(tr):
                gather_row(src_cur, r, 0).start()
            for r in range(tr):
                gather_row(src_cur, r, 0).wait()

        @pl.when(j == 0)
        def _():
            xn_ref[...] = _rms(xbuf[slot], gffn_ref[...]).astype(BF16)
            acc_ref[...] = jnp.zeros_like(acc_ref)

        for c in range(ch):
            gather_row(src_next, j * ch + c, other).start()
        for c in range(ch):
            scatter_row(dst_prev, j * ch + c, other).start()

        acc_ref[...] += _swiglu_chunk(xn_ref[...], wg_ref[...], wu_ref[...], wd_ref[...])

        @pl.when(j == nj - 1)
        def _():
            obuf[slot] = acc_ref[...]
            for r in range(tr):
                gather_row(src_next, r, other).wait()
            for r in range(tr):
                scatter_row(dst_prev, r, other).wait()

            @pl.when(i == nv - 1)
            def _():
                for r in range(tr):
                    scatter_row(dst_cur, r, slot).start()
                for r in range(tr):
                    scatter_row(dst_cur, r, slot).wait()


def _moe_grouped(h1, g_ffn, src, dst, tile_expert, n_valid, wg, wu, wd):
    t = h1.shape[0]
    n_tiles, _, tr = src.shape
    ne, _, f = wg.shape
    tf = f // MOE_CHUNKS
    last = n_tiles - 1
    smem = functools.partial(pl.BlockSpec, memory_space=pltpu.SMEM)
    in_specs = [
        smem((None, 1, tr), lambda i, j, te, nv: (i, 0, 0)),
        smem((None, 1, tr), lambda i, j, te, nv: (jnp.minimum(i + 1, last), 0, 0)),
        smem((None, 1, tr), lambda i, j, te, nv: (i, 0, 0)),
        smem((None, 1, tr), lambda i, j, te, nv: (i + 1, 0, 0)),
        pl.BlockSpec(memory_space=pl.ANY),
        pl.BlockSpec((1, D_MODEL), lambda i, j, te, nv: (0, 0)),
        pl.BlockSpec((None, D_MODEL, tf), lambda i, j, te, nv: (te[i], 0, j)),
        pl.BlockSpec((None, D_MODEL, tf), lambda i, j, te, nv: (te[i], 0, j)),
        pl.BlockSpec((None, tf, D_MODEL), lambda i, j, te, nv: (te[i], j, 0)),
    ]
    grid_spec = pltpu.PrefetchScalarGridSpec(
        num_scalar_prefetch=2, grid=(n_tiles, MOE_CHUNKS), in_specs=in_specs,
        out_specs=pl.BlockSpec(memory_space=pl.ANY),
        scratch_shapes=[pltpu.VMEM((2, tr, D_MODEL), F32), pltpu.VMEM((tr, D_MODEL), BF16),
                        pltpu.VMEM((tr, D_MODEL), F32), pltpu.VMEM((2, tr, D_MODEL), F32),
                        pltpu.SemaphoreType.DMA((2,)), pltpu.SemaphoreType.DMA((2,))])
    return pl.pallas_call(
        _moe_grouped_kernel, grid_spec=grid_spec,
        out_shape=jax.ShapeDtypeStruct((2 * t + (ne + 1) * tr, D_MODEL), F32),
        compiler_params=_params(("arbitrary", "arbitrary")), name="ffn_moe_grouped",
    )(tile_expert, n_valid, src, src, dst, dst, h1, g_ffn, wg, wu, wd)


def _moe_combine_kernel(h1_ref, y1_ref, y2_ref, meta_ref, gfin_ref, o_ref, *, final_norm):
    meta = meta_ref[...]
    lane = lax.broadcasted_iota(jnp.int32, meta.shape, 1)
    g1 = jnp.sum(jnp.where(lane == 2, meta, 0.0), axis=1, keepdims=True)
    g2 = jnp.sum(jnp.where(lane == 3, meta, 0.0), axis=1, keepdims=True)
    out = h1_ref[...] + g1 * y1_ref[...] + g2 * y2_ref[...]
    o_ref[...] = _rms(out, gfin_ref[...]) if final_norm else out


def _moe_combine(h1, y, meta, g_final, tm, final_norm):
    t = h1.shape[0]
    nt = t // tm
    return pl.pallas_call(
        functools.partial(_moe_combine_kernel, final_norm=final_norm), grid=(nt,),
        in_specs=[pl.BlockSpec((tm, D_MODEL), lambda i: (i, 0)), pl.BlockSpec((tm, D_MODEL), lambda i: (i, 0)),
                  pl.BlockSpec((tm, D_MODEL), lambda i: (i + nt, 0)), pl.BlockSpec((tm, LANES), lambda i: (i, 0)),
                  _full((1, D_MODEL))],
        out_specs=pl.BlockSpec((tm, D_MODEL), lambda i: (i, 0)),
        out_shape=jax.ShapeDtypeStruct((t, D_MODEL), F32),
        compiler_params=_params(("parallel",)), name="moe_combine",
    )(h1, y, y, meta, g_final)


def _moe_routed(h1, g_ffn, w_router, wg, wu, wd, g_final, final_norm):
    t = h1.shape[0]
    _, meta = _router(h1, g_ffn, w_router, 1024)
    e1 = meta[:, 0].astype(jnp.int32)
    e2 = meta[:, 1].astype(jnp.int32)
    n_tiles = -(-2 * t // MOE_ROWS) + N_EXPERTS
    src, dst, tile_expert, n_valid = _routing_tables(e1, e2, n_tiles)
    y = _moe_grouped(h1, g_ffn, src, dst, tile_expert, n_valid, wg, wu, wd)
    return _moe_combine(h1, y, meta, g_final, 512, final_norm)


def _prep_layer(l, g_attn_norm, w_in, g_q_norm, w_uq, g_kv_norm, w_uk, w_uv, g_out_sb, g_out_mla, w_out, g_ffn_norm):
    half = MLA_ROPE // 2
    o = 3 * SB_WIDTH + Q_RANK + KV_RANK
    kr = w_in[l][:, o:]
    pad = jnp.zeros((D_MODEL, LANES - MLA_ROPE), F32)
    kr_a = jnp.concatenate([kr, pad], axis=1)
    kr_b = jnp.concatenate([-kr[:, half:], kr[:, :half], pad], axis=1)
    w_in_p = jnp.concatenate([w_in[l][:, :o], kr_a, kr_b], axis=1).astype(BF16)

    uq = w_uq[l]
    w_nope = uq[:, :, :MLA_NOPE].reshape(Q_RANK, MLA_HEADS * MLA_NOPE).astype(BF16)
    qr = uq[:, :, MLA_NOPE:]
    hpad = jnp.zeros((Q_RANK, MLA_HEADS, LANES - MLA_ROPE), F32)
    w_rope_a = jnp.concatenate([qr, hpad], axis=2).reshape(Q_RANK, MLA_HEADS * LANES).astype(BF16)
    w_rope_b = jnp.concatenate([-qr[:, :, half:], qr[:, :, :half], hpad], axis=2)
    w_rope_b = w_rope_b.reshape(Q_RANK, MLA_HEADS * LANES).astype(BF16)

    uk = jnp.transpose(w_uk[l], (1, 2, 0)).reshape(MLA_HEADS // 2, 2, MLA_NOPE, KV_RANK)
    zk = jnp.zeros((MLA_HEADS // 2, MLA_NOPE, KV_RANK), F32)
    w_uk_bd = jnp.concatenate([jnp.concatenate([uk[:, 0], zk], axis=2),
                               jnp.concatenate([zk, uk[:, 1]], axis=2)], axis=1).astype(BF16)
    uv = jnp.transpose(w_uv[l], (1, 0, 2)).reshape(MLA_HEADS // 2, 2, KV_RANK, MLA_V)
    zv = jnp.zeros((MLA_HEADS // 2, KV_RANK, MLA_V), F32)
    w_uv_bd = jnp.concatenate([jnp.concatenate([uv[:, 0], zv], axis=2),
                               jnp.concatenate([zv, uv[:, 1]], axis=2)], axis=1).astype(BF16)
    return {
        "g_attn": g_attn_norm[l][None], "w_in": w_in_p, "g_q": g_q_norm[l][None], "w_nope": w_nope,
        "w_rope_a": w_rope_a, "w_rope_b": w_rope_b, "g_kv": g_kv_norm[l][None], "w_uk_bd": w_uk_bd,
        "w_uv_bd": w_uv_bd, "g_out_sb": g_out_sb[l][None], "g_out_mla": g_out_mla[l][None],
        "w_out": w_out[l].astype(BF16), "g_ffn": g_ffn_norm[l][None],
    }


def _rope_tables(pos):
    inv_freq = 1.0 / (ROPE_THETA ** (jnp.arange(0, MLA_ROPE, 2, dtype=F32) / MLA_ROPE))
    ang = pos.astype(F32)[:, None] * inv_freq[None, :]
    pad = jnp.zeros((pos.shape[0], LANES - MLA_ROPE), F32)
    cos, sin = jnp.cos(ang), jnp.sin(ang)
    return jnp.concatenate([cos, cos, pad], axis=1), jnp.concatenate([sin, sin, pad], axis=1)


def kernel(x_prompt, x_sample, cache_sb_k, cache_sb_v, cache_mla_latent, cache_mla_krope, page_table, g_attn_norm,
           w_in, g_q_norm, w_uq, g_kv_norm, w_uk, w_uv, g_out_sb, g_out_mla, w_out, g_ffn_norm, w_gate_dense,
           w_up_dense, w_down_dense, w_router, w_gate_moe, w_up_moe, w_down_moe, g_final):
    batch, seq, _ = x_prompt.shape
    nb, nq, _ = x_sample.shape
    depth, n_pool = cache_sb_k.shape[:2]
    past_len = page_table.shape[1] * PAGE
    tp, ts = batch * seq, nb * nq

    ckt = jnp.transpose(cache_sb_k, (0, 1, 3, 4, 2)).reshape(depth, n_pool, SB_WIDTH, PAGE)
    cvt = jnp.transpose(cache_sb_v, (0, 1, 3, 4, 2)).reshape(depth, n_pool, SB_WIDTH, PAGE)
    ckrt = jnp.transpose(cache_mla_krope, (0, 1, 3, 2))
    cos_p, sin_p = _rope_tables(jnp.tile(jnp.arange(seq, dtype=jnp.int32), batch))
    cos_s, sin_s = _rope_tables(jnp.tile(past_len + jnp.arange(nq, dtype=jnp.int32), nb))
    g_fin = g_final[None]

    h_p = x_prompt.reshape(tp, D_MODEL)
    h_s = x_sample.reshape(ts, D_MODEL)
    rows_p, rows_s = [], []
    for l in range(depth):
        w = _prep_layer(l, g_attn_norm, w_in, g_q_norm, w_uq, g_kv_norm, w_uk, w_uv, g_out_sb, g_out_mla, w_out,
                        g_ffn_norm)
        last = l == depth - 1

        q, k, v, k16, v16, lat, kr, kvcat, latt, qcat = _proj(h_p, w["g_attn"], w, cos_p, sin_p, 512)
        rows_p.append((k, v, lat, kr))
        o_sb = _sb_attn(q, k16, v16, batch, seq)
        o_mla = _mla_attn(qcat, kvcat, latt, batch, seq)
        h1_p, hn_p = _post(o_sb, o_mla, h_p, w, 512)

        q, k, v, _, _, lat, kr, _, _, qcat = _proj(h_s, w["g_attn"], w, cos_s, sin_s, ts)
        rows_s.append((k, v, lat, kr))
        qcat_s = qcat.reshape(ts // Q_BLOCK, MLA_HEADS, Q_BLOCK // nq, nq, QCAT)
        qcat_s = jnp.transpose(qcat_s, (0, 2, 1, 3, 4)).reshape(nb, MLA_HEADS * nq, QCAT).astype(F32)
        o_sb, o_mla = _decode(l, page_table, q.astype(F32).reshape(nb, nq, SB_WIDTH), k.reshape(nb, nq, SB_WIDTH),
                              v.reshape(nb, nq, SB_WIDTH), qcat_s, lat.reshape(nb, nq, KV_RANK),
                              kr.reshape(nb, nq, MLA_ROPE), ckt, cvt, cache_mla_latent, ckrt)
        o_mla = jnp.transpose(o_mla.reshape(nb, MLA_HEADS, nq, KV_RANK), (0, 2, 1, 3))
        o_mla = o_mla.reshape(ts, MLA_HEADS * KV_RANK).astype(BF16)
        h1_s, hn_s = _post(o_sb.reshape(ts, SB_WIDTH), o_mla, h_s, w, ts)

        i = l // 2
        if l % 2 == 0:
            wg, wu, wd = w_gate_dense[i].astype(BF16), w_up_dense[i].astype(BF16), w_down_dense[i].astype(BF16)
            h_p = _ffn(hn_p, h1_p, wg, wu, wd, g_fin, 1024, 512, last)
            h_s = _ffn(hn_s, h1_s, wg, wu, wd, g_fin, ts, 512, last)
        else:
            wg, wu, wd = w_gate_moe[i].astype(BF16), w_up_moe[i].astype(BF16), w_down_moe[i].astype(BF16)
            wr = jnp.concatenate([w_router[i], jnp.zeros((D_MODEL, LANES - N_EXPERTS), F32)], axis=1)
            h_p = _moe_routed(h1_p, w["g_ffn"], wr, wg, wu, wd, g_fin, last)
            h_s = _moe(hn_s, h1_s, _router(h1_s, w["g_ffn"], wr, ts)[0], wg, wu, wd, g_fin, ts, 512, last)

    y_prompt = h_p.reshape(batch, seq, D_MODEL)
    y_sample = h_s.reshape(nb, nq, D_MODEL)

    def stack(rows, idx, shape):
        return jnp.stack([r[idx].reshape(shape) for r in rows])

    return (y_prompt, y_sample,
            stack(rows_p, 0, (batch, seq, SB_HEADS, HEAD_DIM)), stack(rows_p, 1, (batch, seq, SB_HEADS, HEAD_DIM)),
            stack(rows_p, 2, (batch, seq, KV_RANK)), stack(rows_p, 3, (batch, seq, MLA_ROPE)),
            stack(rows_s, 0, (nb, nq, SB_HEADS, HEAD_DIM)), stack(rows_s, 1, (nb, nq, SB_HEADS, HEAD_DIM)),
            stack(rows_s, 2, (nb, nq, KV_RANK)), stack(rows_s, 3, (nb, nq, MLA_ROPE)))
```

```python
import functools

import jax
import jax.numpy as jnp
from jax import lax
from jax.experimental import pallas as pl
from jax.experimental.pallas import tpu as pltpu

F32 = jnp.float32
BF16 = jnp.bfloat16

D_MODEL = 1024
HEAD_DIM = 64
SB_HEADS = 8
SB_WIDTH = 512
MLA_HEADS = 8
MLA_NOPE = 64
MLA_ROPE = 32
MLA_V = 64
MLA_WIDTH = 512
Q_RANK = 256
KV_RANK = 128
ROPE_THETA = 10000.0
N_EXPERTS = 8
PAGE = 128
Q_BLOCK = 128
EPS = 1e-6
SB_SCALE = HEAD_DIM ** -0.5
MLA_SCALE = (MLA_NOPE + MLA_ROPE) ** -0.5
LOG2E = 1.4426950408889634

LANES = 128
QCAT = 2 * LANES
IN_COLS_PADDED = 3 * SB_WIDTH + Q_RANK + KV_RANK + 2 * LANES
VMEM_LIMIT = 52 * 1024 * 1024
PAGES_PER_STEP = 32
PAGE_GROUP = 8
MOE_CHUNKS = 7
MOE_ROWS = 64 * MOE_CHUNKS
NEG_INF = float("-inf")

_NT = (((1,), (1,)), ((), ()))


def _dot(a, b):
    return jnp.dot(a, b, preferred_element_type=F32)


def _dot_nt(a, b):
    return lax.dot_general(a, b, _NT, preferred_element_type=F32)


def _rms(x, g):
    return x * lax.rsqrt(jnp.mean(x * x, axis=-1, keepdims=True) + EPS) * g


def _params(sem):
    return pltpu.CompilerParams(dimension_semantics=sem, vmem_limit_bytes=VMEM_LIMIT)


def _full(shape):
    return pl.BlockSpec(shape, lambda *_: (0,) * len(shape))


def _proj_kernel(h_ref, g_ref, win_ref, gq_ref, wnope_ref, wra_ref, wrb_ref, gkv_ref, wuk_ref, cos_ref, sin_ref,
                 q_ref, k_ref, v_ref, k16_ref, v16_ref, lat_ref, kr_ref, kvcat_ref, latt_ref, qcat_ref):
    tm = h_ref.shape[0]
    xn = _rms(h_ref[...], g_ref[...]).astype(BF16)
    qkv = _dot(xn, win_ref[:, :3 * SB_WIDTH])
    q_ref[...] = (qkv[:, :SB_WIDTH] * SB_SCALE).astype(BF16)
    k = qkv[:, SB_WIDTH:2 * SB_WIDTH]
    v = qkv[:, 2 * SB_WIDTH:]
    k_ref[...] = k
    v_ref[...] = v
    k16_ref[...] = k.astype(BF16)
    v16_ref[...] = v.astype(BF16)

    rest = _dot(xn, win_ref[:, 3 * SB_WIDTH:])
    cos = cos_ref[...]
    sin = sin_ref[...]
    cqn = _rms(rest[:, :Q_RANK], gq_ref[...]).astype(BF16)
    lat = _rms(rest[:, Q_RANK:Q_RANK + KV_RANK], gkv_ref[...])
    o = Q_RANK + KV_RANK
    krope = rest[:, o:o + LANES] * cos + rest[:, o + LANES:] * sin
    lat_ref[...] = lat
    kr_ref[...] = krope[:, :MLA_ROPE]
    kvcat_ref[:, :LANES] = lat.astype(BF16)
    kvcat_ref[:, LANES:] = krope.astype(BF16)
    for i in range(tm // Q_BLOCK):
        latt_ref[i] = lat[i * Q_BLOCK:(i + 1) * Q_BLOCK].T.astype(BF16)

    qnope = _dot(cqn, wnope_ref[...]).astype(BF16)
    ra = _dot(cqn, wra_ref[...])
    rb = _dot(cqn, wrb_ref[...])
    for p in range(MLA_HEADS // 2):
        ql = _dot(qnope[:, p * LANES:(p + 1) * LANES], wuk_ref[p]).astype(BF16)
        for s in range(2):
            hd = 2 * p + s
            qr = (ra[:, hd * LANES:(hd + 1) * LANES] * cos + rb[:, hd * LANES:(hd + 1) * LANES] * sin).astype(BF16)
            for i in range(tm // Q_BLOCK):
                rows = slice(i * Q_BLOCK, (i + 1) * Q_BLOCK)
                qcat_ref[i, hd, :, :LANES] = ql[rows, s * LANES:(s + 1) * LANES]
                qcat_ref[i, hd, :, LANES:] = qr[rows]


def _proj(h, g, w, cos_t, sin_t, tm):
    t = h.shape[0]
    row = lambda n: pl.BlockSpec((tm, n), lambda i: (i, 0))
    out_shape = (
        jax.ShapeDtypeStruct((t, SB_WIDTH), BF16),
        jax.ShapeDtypeStruct((t, SB_WIDTH), F32),
        jax.ShapeDtypeStruct((t, SB_WIDTH), F32),
        jax.ShapeDtypeStruct((t, SB_WIDTH), BF16),
        jax.ShapeDtypeStruct((t, SB_WIDTH), BF16),
        jax.ShapeDtypeStruct((t, KV_RANK), F32),
        jax.ShapeDtypeStruct((t, MLA_ROPE), F32),
        jax.ShapeDtypeStruct((t, QCAT), BF16),
        jax.ShapeDtypeStruct((t // Q_BLOCK, KV_RANK, Q_BLOCK), BF16),
        jax.ShapeDtypeStruct((t // Q_BLOCK, MLA_HEADS, Q_BLOCK, QCAT), BF16),
    )
    out_specs = (row(SB_WIDTH), row(SB_WIDTH), row(SB_WIDTH), row(SB_WIDTH), row(SB_WIDTH), row(KV_RANK),
                 row(MLA_ROPE), row(QCAT),
                 pl.BlockSpec((tm // Q_BLOCK, KV_RANK, Q_BLOCK), lambda i: (i, 0, 0)),
                 pl.BlockSpec((tm // Q_BLOCK, MLA_HEADS, Q_BLOCK, QCAT), lambda i: (i, 0, 0, 0)))
    in_specs = [row(D_MODEL), _full((1, D_MODEL)), _full((D_MODEL, IN_COLS_PADDED)), _full((1, Q_RANK)),
                _full((Q_RANK, MLA_HEADS * MLA_NOPE)), _full((Q_RANK, MLA_HEADS * LANES)),
                _full((Q_RANK, MLA_HEADS * LANES)), _full((1, KV_RANK)),
                _full((MLA_HEADS // 2, LANES, 2 * LANES)), row(LANES), row(LANES)]
    return pl.pallas_call(
        _proj_kernel, grid=(t // tm,), in_specs=in_specs, out_specs=out_specs, out_shape=out_shape,
        compiler_params=_params(("parallel",)), name="proj",
    )(h, g, w["w_in"], w["g_q"], w["w_nope"], w["w_rope_a"], w["w_rope_b"], w["g_kv"], w["w_uk_bd"], cos_t, sin_t)


def _upper_ones(n):
    r = lax.broadcasted_iota(jnp.int32, (2 * n, 2 * n), 0) % n
    c = lax.broadcasted_iota(jnp.int32, (2 * n, 2 * n), 1)
    return jnp.where(jnp.logical_or(r > c, c >= n), 1.0, 0.0).astype(BF16)


def _sb_log_terms(z, upper_ones, visible):
    n = upper_ones.shape[0] // 2
    sp = jnp.log(1.0 + jnp.exp2(jnp.abs(z) * (-LOG2E)))
    log_beta = jnp.minimum(z, 0.0) - sp
    log_keep = log_beta - z
    if visible is not None:
        log_keep = jnp.where(visible, log_keep, 0.0)
    hi = log_keep.astype(BF16)
    lo = (log_keep - hi.astype(F32)).astype(BF16)
    later, total = [], None
    for t in reversed(range(z.shape[1] // n)):
        cols = slice(t * n, (t + 1) * n)
        sums = _dot(jnp.concatenate([hi[:, cols], lo[:, cols]], axis=1), upper_ones)
        later.insert(0, sums[:, :n] if total is None else sums[:, :n] + total)
        total = sums[:, n:] if total is None else total + sums[:, n:]
    return log_beta, later[0] if len(later) == 1 else jnp.concatenate(later, axis=1), total


def _sb_weights(log_beta, later, visible):
    a = jnp.exp(log_beta + later)
    if visible is not None:
        a = jnp.where(visible, a, 0.0)
    return a.astype(BF16)


def _sb_attn_kernel(q_ref, k_ref, v_ref, o_ref, qm_ref, acc_ref, carry_ref):
    qi = pl.program_id(1)
    pairs = SB_HEADS // 2
    rows = SB_HEADS * Q_BLOCK
    upper_ones = _upper_ones(LANES)
    left = lax.broadcasted_iota(jnp.int32, (Q_BLOCK, LANES), 1) < HEAD_DIM
    for p in range(pairs):
        q2 = q_ref[:, p * LANES:(p + 1) * LANES]
        zero = jnp.zeros_like(q2)
        qm_ref[p] = jnp.concatenate([jnp.where(left, q2, zero), jnp.where(left, zero, q2)], axis=0)

    groups = ((0, 1), (2, 3))
    grows = rows // len(groups)

    def block(kb, nblk, first):
        keys = pl.ds(pl.multiple_of(kb * Q_BLOCK, Q_BLOCK), nblk * Q_BLOCK)
        zs = [jnp.concatenate([_dot_nt(qm_ref[p], k_ref[keys, p * LANES:(p + 1) * LANES]) for p in g], axis=0)
              for g in groups]
        visible = None
        if first:
            r = lax.broadcasted_iota(jnp.int32, (grows, LANES), 0) % Q_BLOCK
            visible = lax.broadcasted_iota(jnp.int32, (grows, LANES), 1) < r
        terms = [_sb_log_terms(z, upper_ones, visible) for z in zs]
        for gi, g in enumerate(groups):
            log_beta, later, total = terms[gi]
            grp = slice(gi * grows, (gi + 1) * grows)
            if first:
                carry_ref[grp] = total
            else:
                carry = carry_ref[grp]
                later = later + jnp.concatenate([carry] * nblk, axis=1)
                carry_ref[grp] = carry + total
            a = _sb_weights(log_beta, later, visible)
            for i, p in enumerate(g):
                pv = _dot(a[i * 2 * Q_BLOCK:(i + 1) * 2 * Q_BLOCK], v_ref[keys, p * LANES:(p + 1) * LANES])
                acc_ref[p] = pv if first else acc_ref[p] + pv

    block(qi, 1, True)

    @pl.when(qi % 2 == 1)
    def _():
        block(qi - 1, 1, False)

    top = qi - qi % 2

    @pl.loop(0, qi // 2)
    def _(j):
        block(top - 2 - 2 * j, 2, False)

    for p in range(pairs):
        o_ref[:, p * LANES:(p + 1) * LANES] = jnp.where(left, acc_ref[p, :Q_BLOCK], acc_ref[p, Q_BLOCK:])


def _sb_attn(q, k16, v16, batch, seq):
    nq = seq // Q_BLOCK
    return pl.pallas_call(
        _sb_attn_kernel, grid=(batch, nq),
        in_specs=[pl.BlockSpec((Q_BLOCK, SB_WIDTH), lambda b, i: (b * nq + i, 0)),
                  pl.BlockSpec((seq, SB_WIDTH), lambda b, i: (b, 0)),
                  pl.BlockSpec((seq, SB_WIDTH), lambda b, i: (b, 0))],
        out_specs=pl.BlockSpec((Q_BLOCK, SB_WIDTH), lambda b, i: (b * nq + i, 0)),
        out_shape=jax.ShapeDtypeStruct((batch * seq, SB_WIDTH), F32),
        scratch_shapes=[pltpu.VMEM((SB_HEADS // 2, 2 * Q_BLOCK, LANES), BF16),
                        pltpu.VMEM((SB_HEADS // 2, 2 * Q_BLOCK, LANES), F32),
                        pltpu.VMEM((SB_HEADS * Q_BLOCK, LANES), F32)],
        compiler_params=_params(("parallel", "parallel")), name="sb_attn",
    )(q, k16, v16)


def _mla_attn_kernel(q_ref, kv_ref, latt_ref, o_ref, m_ref, l_ref, acc_ref):
    qi = pl.program_id(1)
    cols = MLA_HEADS * Q_BLOCK
    q = q_ref[0].reshape(cols, QCAT)
    key = lax.broadcasted_iota(jnp.int32, (Q_BLOCK, cols), 0)
    qry = lax.broadcasted_iota(jnp.int32, (Q_BLOCK, cols), 1) % Q_BLOCK

    def scores(start, size):
        kv = kv_ref[pl.ds(pl.multiple_of(start, Q_BLOCK), size), :]
        return _dot_nt(kv, q) * (MLA_SCALE * LOG2E)

    s = jnp.where(key <= qry, scores(qi * Q_BLOCK, Q_BLOCK), NEG_INF)
    m = jnp.max(s, axis=0, keepdims=True)
    p = jnp.exp2(s - m)
    m_ref[...] = m
    l_ref[...] = jnp.sum(p, axis=0, keepdims=True)
    acc_ref[...] = _dot(latt_ref[qi], p.astype(BF16))

    halves = (slice(0, cols // 2), slice(cols // 2, cols))

    def update(start, size, latt):
        kv = kv_ref[pl.ds(pl.multiple_of(start, Q_BLOCK), size), :]
        ss = [_dot_nt(kv, q[h]) * (MLA_SCALE * LOG2E) for h in halves]
        for h, s in zip(halves, ss):
            m_prev = m_ref[:, h]
            m_new = jnp.maximum(m_prev, jnp.max(s, axis=0, keepdims=True))
            alpha = jnp.exp2(m_prev - m_new)
            p = jnp.exp2(s - m_new)
            l_ref[:, h] = alpha * l_ref[:, h] + jnp.sum(p, axis=0, keepdims=True)
            acc_ref[:, h] = alpha * acc_ref[:, h] + _dot(latt, p.astype(BF16))
            m_ref[:, h] = m_new

    @pl.when(qi % 2 == 1)
    def _():
        update((qi - 1) * Q_BLOCK, Q_BLOCK, latt_ref[qi - 1])

    @pl.loop(0, qi // 2)
    def _(j):
        latt = jnp.concatenate([latt_ref[2 * j], latt_ref[2 * j + 1]], axis=1)
        update(j * 2 * Q_BLOCK, 2 * Q_BLOCK, latt)

    out = acc_ref[...] / l_ref[...]
    for hd in range(MLA_HEADS):
        blk = slice(hd * Q_BLOCK, (hd + 1) * Q_BLOCK)
        o_ref[:, hd * KV_RANK:(hd + 1) * KV_RANK] = out[:, blk].T.astype(BF16)


def _mla_attn(qcat, kvcat, latt, batch, seq):
    nq = seq // Q_BLOCK
    cols = MLA_HEADS * Q_BLOCK
    return pl.pallas_call(
        _mla_attn_kernel, grid=(batch, nq),
        in_specs=[pl.BlockSpec((1, MLA_HEADS, Q_BLOCK, QCAT), lambda b, i: (b * nq + i, 0, 0, 0)),
                  pl.BlockSpec((seq, QCAT), lambda b, i: (b, 0)),
                  pl.BlockSpec((nq, KV_RANK, Q_BLOCK), lambda b, i: (b, 0, 0))],
        out_specs=pl.BlockSpec((Q_BLOCK, MLA_HEADS * KV_RANK), lambda b, i: (b * nq + i, 0)),
        out_shape=jax.ShapeDtypeStruct((batch * seq, MLA_HEADS * KV_RANK), BF16),
        scratch_shapes=[pltpu.VMEM((1, cols), F32), pltpu.VMEM((1, cols), F32), pltpu.VMEM((KV_RANK, cols), F32)],
        compiler_params=_params(("parallel", "parallel")), name="mla_attn",
    )(qcat, kvcat, latt)


def _decode_kernel(pt_ref, qsb_ref, knew_ref, vnew_ref, qcat_ref, latnew_ref, krnew_ref, *refs):
    n = PAGES_PER_STEP
    kt_refs, vt_refs, lat_refs, krt_refs = refs[:n], refs[n:2 * n], refs[2 * n:3 * n], refs[3 * n:4 * n]
    osb_ref, omla_ref = refs[4 * n:4 * n + 2]
    acc_sb, carry_ref, m_ref, l_ref, acc_mla = refs[4 * n + 2:]
    step = pl.program_id(1)
    nq = qsb_ref.shape[0]
    rows = SB_HEADS * nq

    q_tiled = jnp.concatenate([qsb_ref[...]] * SB_HEADS, axis=0)
    row_head = lax.broadcasted_iota(jnp.int32, (rows, SB_WIDTH), 0) // nq
    lane_head = lax.broadcasted_iota(jnp.int32, (rows, SB_WIDTH), 1) // HEAD_DIM
    own_head = row_head == lane_head
    q_bd = jnp.where(own_head, q_tiled, 0.0).astype(BF16)
    qcat = qcat_ref[...].astype(BF16)
    q_lat = qcat[:, :KV_RANK]
    q_rope = qcat[:, KV_RANK:KV_RANK + MLA_ROPE]
    upper_ones = _upper_ones(PAGE)
    scale = MLA_SCALE * LOG2E

    @pl.when(step == 0)
    def _():
        pad = PAGE - nq
        q_idx = lax.broadcasted_iota(jnp.int32, (rows, PAGE), 0) % nq
        key_idx = lax.broadcasted_iota(jnp.int32, (rows, PAGE), 1)
        knew = jnp.concatenate([knew_ref[...], jnp.zeros((pad, SB_WIDTH), F32)], axis=0).astype(BF16)
        vnew = jnp.concatenate([vnew_ref[...], jnp.zeros((pad, SB_WIDTH), F32)], axis=0).astype(BF16)
        visible = key_idx < q_idx
        log_beta, later, total = _sb_log_terms(_dot_nt(q_bd, knew), upper_ones, visible)
        carry_ref[...] = total
        acc_sb[...] = _dot(_sb_weights(log_beta, later, visible), vnew)
        lat = jnp.concatenate([latnew_ref[...], jnp.zeros((pad, KV_RANK), F32)], axis=0).astype(BF16)
        kr = jnp.concatenate([krnew_ref[...], jnp.zeros((pad, MLA_ROPE), F32)], axis=0).astype(BF16)
        s = jnp.where(key_idx <= q_idx, (_dot_nt(q_lat, lat) + _dot_nt(q_rope, kr)) * scale, NEG_INF)
        m = jnp.max(s, axis=1, keepdims=True)
        p = jnp.exp2(s - m)
        m_ref[...] = m
        l_ref[...] = jnp.sum(p, axis=1, keepdims=True)
        acc_mla[...] = _dot(p.astype(BF16), lat)

    halves = [range(g, g + PAGE_GROUP) for g in range(0, n, PAGE_GROUP)]
    zs = [jnp.concatenate([_dot(q_bd, kt_refs[j][...].astype(BF16)) for j in half], axis=0) for half in halves]
    lats = [lat_refs[j][...].astype(BF16) for j in range(n)]
    s = jnp.concatenate([_dot_nt(q_lat, lats[j]) + _dot(q_rope, krt_refs[j][...].astype(BF16)) for j in range(n)],
                        axis=1) * scale
    terms = [_sb_log_terms(z, upper_ones, None) for z in zs]
    carry = carry_ref[...]
    acc = acc_sb[...]
    for half, (log_beta, later, total) in zip(halves, terms):
        for i, j in enumerate(half):
            page = slice(i * rows, (i + 1) * rows)
            a = _sb_weights(log_beta[page], later[page] + carry, None)
            carry = carry + total[page]
            acc = acc + _dot_nt(a, vt_refs[j][...].astype(BF16))
    carry_ref[...] = carry
    acc_sb[...] = acc

    m = m_ref[...]
    m_new = jnp.maximum(m, jnp.max(s, axis=1, keepdims=True))
    alpha = jnp.exp2(m - m_new)
    p = jnp.exp2(s - m_new)
    l = alpha * l_ref[...] + jnp.sum(p, axis=1, keepdims=True)
    p = p.astype(BF16)
    accm = alpha * acc_mla[...]
    for j in range(n):
        accm = accm + _dot(p[:, j * PAGE:(j + 1) * PAGE], lats[j])
    m_ref[...] = m_new
    l_ref[...] = l
    acc_mla[...] = accm

    @pl.when(step == pl.num_programs(1) - 1)
    def _():
        own = jnp.where(own_head, acc, 0.0)
        out = own[:nq]
        for hd in range(1, SB_HEADS):
            out = out + own[hd * nq:(hd + 1) * nq]
        osb_ref[...] = out
        omla_ref[...] = accm / l


def _decode(layer, page_table, qsb, knew, vnew, qcat, latnew, krnew, ckt, cvt, clat, ckrt):
    nb, nq, _ = qsb.shape
    n_pages = page_table.shape[1]
    n = PAGES_PER_STEP
    steps = n_pages // n
    rows = SB_HEADS * nq

    def per_seq(shape):
        return pl.BlockSpec((None,) + shape, lambda b, s, pt: (b, 0, 0))

    def page_spec(shape, j):
        return pl.BlockSpec((None, None) + shape,
                            lambda b, s, pt, j=j: (layer, pt[b, n_pages - 1 - (s * n + j)], 0, 0))

    in_specs = [per_seq((nq, SB_WIDTH)), per_seq((nq, SB_WIDTH)), per_seq((nq, SB_WIDTH)), per_seq((rows, QCAT)),
                per_seq((nq, KV_RANK)), per_seq((nq, MLA_ROPE))]
    for shape in ((SB_WIDTH, PAGE), (SB_WIDTH, PAGE), (PAGE, KV_RANK), (MLA_ROPE, PAGE)):
        in_specs += [page_spec(shape, j) for j in range(n)]
    grid_spec = pltpu.PrefetchScalarGridSpec(
        num_scalar_prefetch=1, grid=(nb, steps), in_specs=in_specs,
        out_specs=(per_seq((nq, SB_WIDTH)), per_seq((rows, KV_RANK))),
        scratch_shapes=[pltpu.VMEM((rows, SB_WIDTH), F32), pltpu.VMEM((rows, PAGE), F32), pltpu.VMEM((rows, 1), F32),
                        pltpu.VMEM((rows, 1), F32), pltpu.VMEM((rows, KV_RANK), F32)])
    return pl.pallas_call(
        _decode_kernel, grid_spec=grid_spec,
        out_shape=(jax.ShapeDtypeStruct((nb, nq, SB_WIDTH), F32), jax.ShapeDtypeStruct((nb, rows, KV_RANK), F32)),
        compiler_params=_params(("parallel", "arbitrary")), name="decode_attn",
    )(page_table, qsb, knew, vnew, qcat, latnew, krnew, *([ckt] * n), *([cvt] * n), *([clat] * n), *([ckrt] * n))


def _post_kernel(osb_ref, omla_ref, h_ref, gsb_ref, gmla_ref, wuv_ref, wout_ref, gffn_ref, h1_ref, hn_ref):
    pairs = [_dot(omla_ref[:, p * 2 * KV_RANK:(p + 1) * 2 * KV_RANK], wuv_ref[p]) for p in range(MLA_HEADS // 2)]
    o_mla = jnp.concatenate(pairs, axis=1)
    mixed = jnp.concatenate([_rms(osb_ref[...], gsb_ref[...]), _rms(o_mla, gmla_ref[...])], axis=1).astype(BF16)
    h1 = h_ref[...] + _dot(mixed, wout_ref[...])
    h1_ref[...] = h1
    hn_ref[...] = _rms(h1, gffn_ref[...]).astype(BF16)


def _post(osb, omla, h, w, tm):
    t = h.shape[0]
    row = lambda n: pl.BlockSpec((tm, n), lambda i: (i, 0))
    return pl.pallas_call(
        _post_kernel, grid=(t // tm,),
        in_specs=[row(SB_WIDTH), row(MLA_HEADS * KV_RANK), row(D_MODEL), _full((1, SB_WIDTH)), _full((1, MLA_WIDTH)),
                  _full((MLA_HEADS // 2, 2 * KV_RANK, 2 * MLA_V)), _full((D_MODEL, D_MODEL)), _full((1, D_MODEL))],
        out_specs=(row(D_MODEL), row(D_MODEL)),
        out_shape=(jax.ShapeDtypeStruct((t, D_MODEL), F32), jax.ShapeDtypeStruct((t, D_MODEL), BF16)),
        compiler_params=_params(("parallel",)), name="post_attn",
    )(osb, omla, h, w["g_out_sb"], w["g_out_mla"], w["w_uv_bd"], w["w_out"], w["g_ffn"])


def _swiglu_chunk(x, wg, wu, wd):
    g = _dot(x, wg)
    act = (g * jax.nn.sigmoid(g) * _dot(x, wu)).astype(BF16)
    return _dot(act, wd)


def _ffn_kernel(hn_ref, h1_ref, wg_ref, wu_ref, wd_ref, gfin_ref, o_ref, acc_ref, *, final_norm):
    j = pl.program_id(1)

    @pl.when(j == 0)
    def _():
        acc_ref[...] = jnp.zeros_like(acc_ref)

    acc_ref[...] += _swiglu_chunk(hn_ref[...], wg_ref[...], wu_ref[...], wd_ref[...])

    @pl.when(j == pl.num_programs(1) - 1)
    def _():
        out = h1_ref[...] + acc_ref[...]
        o_ref[...] = _rms(out, gfin_ref[...]) if final_norm else out


def _ffn(hn, h1, wg, wu, wd, g_final, tm, tf, final_norm):
    t = hn.shape[0]
    f = wg.shape[1]
    return pl.pallas_call(
        functools.partial(_ffn_kernel, final_norm=final_norm), grid=(t // tm, f // tf),
        in_specs=[pl.BlockSpec((tm, D_MODEL), lambda i, j: (i, 0)), pl.BlockSpec((tm, D_MODEL), lambda i, j: (i, 0)),
                  pl.BlockSpec((D_MODEL, tf), lambda i, j: (0, j)), pl.BlockSpec((D_MODEL, tf), lambda i, j: (0, j)),
                  pl.BlockSpec((tf, D_MODEL), lambda i, j: (j, 0)), _full((1, D_MODEL))],
        out_specs=pl.BlockSpec((tm, D_MODEL), lambda i, j: (i, 0)),
        out_shape=jax.ShapeDtypeStruct((t, D_MODEL), F32),
        scratch_shapes=[pltpu.VMEM((tm, D_MODEL), F32)],
        compiler_params=_params(("parallel", "arbitrary")), name="ffn_dense",
    )(hn, h1, wg, wu, wd, g_final)


def _router_kernel(h1_ref, gffn_ref, wr_ref, comb_ref, meta_ref):
    hn = _rms(h1_ref[...], gffn_ref[...])
    logits = jnp.dot(hn, wr_ref[...], precision=lax.Precision.HIGHEST, preferred_element_type=F32)
    lane = lax.broadcasted_iota(jnp.int32, logits.shape, 1)
    logits = jnp.where(lane < N_EXPERTS, logits, NEG_INF)
    m1 = jnp.max(logits, axis=1, keepdims=True)
    i1 = jnp.min(jnp.where(logits == m1, lane, LANES), axis=1, keepdims=True)
    rest = jnp.where(lane == i1, NEG_INF, logits)
    m2 = jnp.max(rest, axis=1, keepdims=True)
    i2 = jnp.min(jnp.where(rest == m2, lane, LANES), axis=1, keepdims=True)
    e2 = jnp.exp(m2 - m1)
    g1 = 1.0 / (1.0 + e2)
    g2 = e2 / (1.0 + e2)
    comb_ref[...] = jnp.where(lane == i1, g1, 0.0) + jnp.where(lane == i2, g2, 0.0)
    meta_ref[...] = jnp.where(lane == 0, i1.astype(F32), jnp.where(lane == 1, i2.astype(F32),
                              jnp.where(lane == 2, g1, jnp.where(lane == 3, g2, 0.0))))


def _router(h1, g_ffn, w_router, tm):
    t = h1.shape[0]
    return pl.pallas_call(
        _router_kernel, grid=(t // tm,),
        in_specs=[pl.BlockSpec((tm, D_MODEL), lambda i: (i, 0)), _full((1, D_MODEL)), _full((D_MODEL, LANES))],
        out_specs=(pl.BlockSpec((tm, LANES), lambda i: (i, 0)), pl.BlockSpec((tm, LANES), lambda i: (i, 0))),
        out_shape=(jax.ShapeDtypeStruct((t, LANES), F32), jax.ShapeDtypeStruct((t, LANES), F32)),
        compiler_params=_params(("parallel",)), name="router",
    )(h1, g_ffn, w_router)


def _moe_kernel(hn_ref, h1_ref, comb_ref, wg_ref, wu_ref, wd_ref, gfin_ref, o_ref, acc_ref, *, final_norm):
    e = pl.program_id(1)
    j = pl.program_id(2)

    @pl.when(jnp.logical_and(e == 0, j == 0))
    def _():
        acc_ref[...] = jnp.zeros_like(acc_ref)

    lane = lax.broadcasted_iota(jnp.int32, comb_ref.shape, 1)
    gate = jnp.sum(jnp.where(lane == e, comb_ref[...], 0.0), axis=1, keepdims=True)
    acc_ref[...] += gate * _swiglu_chunk(hn_ref[...], wg_ref[...], wu_ref[...], wd_ref[...])

    @pl.when(jnp.logical_and(e == pl.num_programs(1) - 1, j == pl.num_programs(2) - 1))
    def _():
        out = h1_ref[...] + acc_ref[...]
        o_ref[...] = _rms(out, gfin_ref[...]) if final_norm else out


def _moe(hn, h1, comb, wg, wu, wd, g_final, tm, tf, final_norm):
    t = hn.shape[0]
    ne, _, f = wg.shape
    tok = lambda n: pl.BlockSpec((tm, n), lambda i, e, j: (i, 0))
    return pl.pallas_call(
        functools.partial(_moe_kernel, final_norm=final_norm), grid=(t // tm, ne, f // tf),
        in_specs=[tok(D_MODEL), tok(D_MODEL), tok(LANES),
                  pl.BlockSpec((None, D_MODEL, tf), lambda i, e, j: (e, 0, j)),
                  pl.BlockSpec((None, D_MODEL, tf), lambda i, e, j: (e, 0, j)),
                  pl.BlockSpec((None, tf, D_MODEL), lambda i, e, j: (e, j, 0)), _full((1, D_MODEL))],
        out_specs=tok(D_MODEL),
        out_shape=jax.ShapeDtypeStruct((t, D_MODEL), F32),
        scratch_shapes=[pltpu.VMEM((tm, D_MODEL), F32)],
        compiler_params=_params(("parallel", "arbitrary", "arbitrary")), name="ffn_moe",
    )(hn, h1, comb, wg, wu, wd, g_final)


def _routing_tables(e1, e2, n_tiles):
    t = e1.shape[0]
    tr = MOE_ROWS
    flat = jnp.stack([e1, e2], axis=1).reshape(2 * t)
    order = jnp.argsort(flat, stable=True).astype(jnp.int32)
    counts = jnp.sum(flat[:, None] == jnp.arange(N_EXPERTS, dtype=jnp.int32)[None, :], axis=0).astype(jnp.int32)
    padded = (counts + tr - 1) // tr * tr
    pend = jnp.cumsum(padded)
    pstart = pend - padded
    ustart = jnp.cumsum(counts) - counts
    n_valid = (pend[-1] // tr).astype(jnp.int32)
    tile_start = jnp.arange(n_tiles, dtype=jnp.int32) * tr
    tile_expert = jnp.minimum(jnp.sum(tile_start[:, None] >= pend[None, :], axis=1), N_EXPERTS - 1).astype(jnp.int32)
    slot = jnp.arange(n_tiles * tr, dtype=jnp.int32)
    e = jnp.repeat(tile_expert, tr)
    rank = slot - pstart[e]
    valid = jnp.logical_and(slot < pend[-1], rank < counts[e])
    a = order[jnp.clip(ustart[e] + rank, 0, 2 * t - 1)]
    src = jnp.where(valid, a // 2, 0)
    dst = jnp.where(valid, (a % 2) * t + a // 2, 2 * t + e * tr + jnp.clip(rank - counts[e], 0, tr - 1))
    dst = jnp.concatenate([2 * t + N_EXPERTS * tr + jnp.arange(tr, dtype=jnp.int32), dst])
    return src.reshape(n_tiles, 1, tr), dst.reshape(n_tiles + 1, 1, tr), tile_expert, n_valid.reshape(1)


def _moe_grouped_kernel(te_ref, nv_ref, src_cur, src_next, dst_prev, dst_cur, h1_hbm, gffn_ref, wg_ref, wu_ref, wd_ref,
                        y_hbm, xbuf, xn_ref, acc_ref, obuf, gsem, ssem):
    i = pl.program_id(0)
    j = pl.program_id(1)
    nj = pl.num_programs(1)
    nv = nv_ref[0]
    tr = xn_ref.shape[0]
    ch = tr // MOE_CHUNKS
    slot = i % 2
    other = 1 - slot

    def gather_row(tok_ref, r, s):
        return pltpu.make_async_copy(h1_hbm.at[pl.ds(tok_ref[0, r], 1)], xbuf.at[s, pl.ds(r, 1)], gsem.at[s])

    def scatter_row(row_ref, r, s):
        return pltpu.make_async_copy(obuf.at[s, pl.ds(r, 1)], y_hbm.at[pl.ds(row_ref[0, r], 1)], ssem.at[s])

    @pl.when(i < nv)
    def _():
        @pl.when(jnp.logical_and(i == 0, j == 0))
        def _():
            obuf[...] = jnp.zeros_like(obuf)
            for r in range(tr):
                gather_row(src_cur, r, 0).start()
            for r in range(tr):
                gather_row(src_cur, r, 0).wait()

        @pl.when(j == 0)
        def _():
            xn_ref[...] = _rms(xbuf[slot], gffn_ref[...]).astype(BF16)
            acc_ref[...] = jnp.zeros_like(acc_ref)

        for c in range(ch):
            gather_row(src_next, j * ch + c, other).start()
        for c in range(ch):
            scatter_row(dst_prev, j * ch + c, other).start()

        acc_ref[...] += _swiglu_chunk(xn_ref[...], wg_ref[...], wu_ref[...], wd_ref[...])

        @pl.when(j == nj - 1)
        def _():
            obuf[slot] = acc_ref[...]
            for r in range(tr):
                gather_row(src_next, r, other).wait()
            for r in range(tr):
                scatter_row(dst_prev, r, other).wait()

            @pl.when(i == nv - 1)
            def _():
                for r in range(tr):
                    scatter_row(dst_cur, r, slot).start()
                for r in range(tr):
                    scatter_row(dst_cur, r, slot).wait()


def _moe_grouped(h1, g_ffn, src, dst, tile_expert, n_valid, wg, wu, wd):
    t = h1.shape[0]
    n_tiles, _, tr = src.shape
    ne, _, f = wg.shape
    tf = f // MOE_CHUNKS
    last = n_tiles - 1
    smem = functools.partial(pl.BlockSpec, memory_space=pltpu.SMEM)
    in_specs = [
        smem((None, 1, tr), lambda i, j, te, nv: (i, 0, 0)),
        smem((None, 1, tr), lambda i, j, te, nv: (jnp.minimum(i + 1, last), 0, 0)),
        smem((None, 1, tr), lambda i, j, te, nv: (i, 0, 0)),
        smem((None, 1, tr), lambda i, j, te, nv: (i + 1, 0, 0)),
        pl.BlockSpec(memory_space=pl.ANY),
        pl.BlockSpec((1, D_MODEL), lambda i, j, te, nv: (0, 0)),
        pl.BlockSpec((None, D_MODEL, tf), lambda i, j, te, nv: (te[i], 0, j)),
        pl.BlockSpec((None, D_MODEL, tf), lambda i, j, te, nv: (te[i], 0, j)),
        pl.BlockSpec((None, tf, D_MODEL), lambda i, j, te, nv: (te[i], j, 0)),
    ]
    grid_spec = pltpu.PrefetchScalarGridSpec(
        num_scalar_prefetch=2, grid=(n_tiles, MOE_CHUNKS), in_specs=in_specs,
        out_specs=pl.BlockSpec(memory_space=pl.ANY),
        scratch_shapes=[pltpu.VMEM((2, tr, D_MODEL), F32), pltpu.VMEM((tr, D_MODEL), BF16),
                        pltpu.VMEM((tr, D_MODEL), F32), pltpu.VMEM((2, tr, D_MODEL), F32),
                        pltpu.SemaphoreType.DMA((2,)), pltpu.SemaphoreType.DMA((2,))])
    return pl.pallas_call(
        _moe_grouped_kernel, grid_spec=grid_spec,
        out_shape=jax.ShapeDtypeStruct((2 * t + (ne + 1) * tr, D_MODEL), F32),
        compiler_params=_params(("arbitrary", "arbitrary")), name="ffn_moe_grouped",
    )(tile_expert, n_valid, src, src, dst, dst, h1, g_ffn, wg, wu, wd)


def _moe_combine_kernel(h1_ref, y1_ref, y2_ref, meta_ref, gfin_ref, o_ref, *, final_norm):
    meta = meta_ref[...]
    lane = lax.broadcasted_iota(jnp.int32, meta.shape, 1)
    g1 = jnp.sum(jnp.where(lane == 2, meta, 0.0), axis=1, keepdims=True)
    g2 = jnp.sum(jnp.where(lane == 3, meta, 0.0), axis=1, keepdims=True)
    out = h1_ref[...] + g1 * y1_ref[...] + g2 * y2_ref[...]
    o_ref[...] = _rms(out, gfin_ref[...]) if final_norm else out


def _moe_combine(h1, y, meta, g_final, tm, final_norm):
    t = h1.shape[0]
    nt = t // tm
    return pl.pallas_call(
        functools.partial(_moe_combine_kernel, final_norm=final_norm), grid=(nt,),
        in_specs=[pl.BlockSpec((tm, D_MODEL), lambda i: (i, 0)), pl.BlockSpec((tm, D_MODEL), lambda i: (i, 0)),
                  pl.BlockSpec((tm, D_MODEL), lambda i: (i + nt, 0)), pl.BlockSpec((tm, LANES), lambda i: (i, 0)),
                  _full((1, D_MODEL))],
        out_specs=pl.BlockSpec((tm, D_MODEL), lambda i: (i, 0)),
        out_shape=jax.ShapeDtypeStruct((t, D_MODEL), F32),
        compiler_params=_params(("parallel",)), name="moe_combine",
    )(h1, y, y, meta, g_final)


def _moe_routed(h1, g_ffn, w_router, wg, wu, wd, g_final, final_norm):
    t = h1.shape[0]
    _, meta = _router(h1, g_ffn, w_router, 1024)
    e1 = meta[:, 0].astype(jnp.int32)
    e2 = meta[:, 1].astype(jnp.int32)
    n_tiles = -(-2 * t // MOE_ROWS) + N_EXPERTS
    src, dst, tile_expert, n_valid = _routing_tables(e1, e2, n_tiles)
    y = _moe_grouped(h1, g_ffn, src, dst, tile_expert, n_valid, wg, wu, wd)
    return _moe_combine(h1, y, meta, g_final, 512, final_norm)


def _prep_layer(l, g_attn_norm, w_in, g_q_norm, w_uq, g_kv_norm, w_uk, w_uv, g_out_sb, g_out_mla, w_out, g_ffn_norm):
    half = MLA_ROPE // 2
    o = 3 * SB_WIDTH + Q_RANK + KV_RANK
    kr = w_in[l][:, o:]
    pad = jnp.zeros((D_MODEL, LANES - MLA_ROPE), F32)
    kr_a = jnp.concatenate([kr, pad], axis=1)
    kr_b = jnp.concatenate([-kr[:, half:], kr[:, :half], pad], axis=1)
    w_in_p = jnp.concatenate([w_in[l][:, :o], kr_a, kr_b], axis=1).astype(BF16)

    uq = w_uq[l]
    w_nope = uq[:, :, :MLA_NOPE].reshape(Q_RANK, MLA_HEADS * MLA_NOPE).astype(BF16)
    qr = uq[:, :, MLA_NOPE:]
    hpad = jnp.zeros((Q_RANK, MLA_HEADS, LANES - MLA_ROPE), F32)
    w_rope_a = jnp.concatenate([qr, hpad], axis=2).reshape(Q_RANK, MLA_HEADS * LANES).astype(BF16)
    w_rope_b = jnp.concatenate([-qr[:, :, half:], qr[:, :, :half], hpad], axis=2)
    w_rope_b = w_rope_b.reshape(Q_RANK, MLA_HEADS * LANES).astype(BF16)

    uk = jnp.transpose(w_uk[l], (1, 2, 0)).reshape(MLA_HEADS // 2, 2, MLA_NOPE, KV_RANK)
    zk = jnp.zeros((MLA_HEADS // 2, MLA_NOPE, KV_RANK), F32)
    w_uk_bd = jnp.concatenate([jnp.concatenate([uk[:, 0], zk], axis=2),
                               jnp.concatenate([zk, uk[:, 1]], axis=2)], axis=1).astype(BF16)
    uv = jnp.transpose(w_uv[l], (1, 0, 2)).reshape(MLA_HEADS // 2, 2, KV_RANK, MLA_V)
    zv = jnp.zeros((MLA_HEADS // 2, KV_RANK, MLA_V), F32)
    w_uv_bd = jnp.concatenate([jnp.concatenate([uv[:, 0], zv], axis=2),
                               jnp.concatenate([zv, uv[:, 1]], axis=2)], axis=1).astype(BF16)
    return {
        "g_attn": g_attn_norm[l][None], "w_in": w_in_p, "g_q": g_q_norm[l][None], "w_nope": w_nope,
        "w_rope_a": w_rope_a, "w_rope_b": w_rope_b, "g_kv": g_kv_norm[l][None], "w_uk_bd": w_uk_bd,
        "w_uv_bd": w_uv_bd, "g_out_sb": g_out_sb[l][None], "g_out_mla": g_out_mla[l][None],
        "w_out": w_out[l].astype(BF16), "g_ffn": g_ffn_norm[l][None],
    }


def _rope_tables(pos):
    inv_freq = 1.0 / (ROPE_THETA ** (jnp.arange(0, MLA_ROPE, 2, dtype=F32) / MLA_ROPE))
    ang = pos.astype(F32)[:, None] * inv_freq[None, :]
    pad = jnp.zeros((pos.shape[0], LANES - MLA_ROPE), F32)
    cos, sin = jnp.cos(ang), jnp.sin(ang)
    return jnp.concatenate([cos, cos, pad], axis=1), jnp.concatenate([sin, sin, pad], axis=1)


def kernel(x_prompt, x_sample, cache_sb_k, cache_sb_v, cache_mla_latent, cache_mla_krope, page_table, g_attn_norm,
           w_in, g_q_norm, w_uq, g_kv_norm, w_uk, w_uv, g_out_sb, g_out_mla, w_out, g_ffn_norm, w_gate_dense,
           w_up_dense, w_down_dense, w_router, w_gate_moe, w_up_moe, w_down_moe, g_final):
    batch, seq, _ = x_prompt.shape
    nb, nq, _ = x_sample.shape
    depth, n_pool = cache_sb_k.shape[:2]
    past_len = page_table.shape[1] * PAGE
    tp, ts = batch * seq, nb * nq

    ckt = jnp.transpose(cache_sb_k, (0, 1, 3, 4, 2)).reshape(depth, n_pool, SB_WIDTH, PAGE)
    cvt = jnp.transpose(cache_sb_v, (0, 1, 3, 4, 2)).reshape(depth, n_pool, SB_WIDTH, PAGE)
    ckrt = jnp.transpose(cache_mla_krope, (0, 1, 3, 2))
    cos_p, sin_p = _rope_tables(jnp.tile(jnp.arange(seq, dtype=jnp.int32), batch))
    cos_s, sin_s = _rope_tables(jnp.tile(past_len + jnp.arange(nq, dtype=jnp.int32), nb))
    g_fin = g_final[None]

    h_p = x_prompt.reshape(tp, D_MODEL)
    h_s = x_sample.reshape(ts, D_MODEL)
    rows_p, rows_s = [], []
    for l in range(depth):
        w = _prep_layer(l, g_attn_norm, w_in, g_q_norm, w_uq, g_kv_norm, w_uk, w_uv, g_out_sb, g_out_mla, w_out,
                        g_ffn_norm)
        last = l == depth - 1

        q, k, v, k16, v16, lat, kr, kvcat, latt, qcat = _proj(h_p, w["g_attn"], w, cos_p, sin_p, 512)
        rows_p.append((k, v, lat, kr))
        o_sb = _sb_attn(q, k16, v16, batch, seq)
        o_mla = _mla_attn(qcat, kvcat, latt, batch, seq)
        h1_p, hn_p = _post(o_sb, o_mla, h_p, w, 512)

        q, k, v, _, _, lat, kr, _, _, qcat = _proj(h_s, w["g_attn"], w, cos_s, sin_s, ts)
        rows_s.append((k, v, lat, kr))
        qcat_s = qcat.reshape(ts // Q_BLOCK, MLA_HEADS, Q_BLOCK // nq, nq, QCAT)
        qcat_s = jnp.transpose(qcat_s, (0, 2, 1, 3, 4)).reshape(nb, MLA_HEADS * nq, QCAT).astype(F32)
        o_sb, o_mla = _decode(l, page_table, q.astype(F32).reshape(nb, nq, SB_WIDTH), k.reshape(nb, nq, SB_WIDTH),
                              v.reshape(nb, nq, SB_WIDTH), qcat_s, lat.reshape(nb, nq, KV_RANK),
                              kr.reshape(nb, nq, MLA_ROPE), ckt, cvt, cache_mla_latent, ckrt)
        o_mla = jnp.transpose(o_mla.reshape(nb, MLA_HEADS, nq, KV_RANK), (0, 2, 1, 3))
        o_mla = o_mla.reshape(ts, MLA_HEADS * KV_RANK).astype(BF16)
        h1_s, hn_s = _post(o_sb.reshape(ts, SB_WIDTH), o_mla, h_s, w, ts)

        i = l // 2
        if l % 2 == 0:
            wg, wu, wd = w_gate_dense[i].astype(BF16), w_up_dense[i].astype(BF16), w_down_dense[i].astype(BF16)
            h_p = _ffn(hn_p, h1_p, wg, wu, wd, g_fin, 1024, 512, last)
            h_s = _ffn(hn_s, h1_s, wg, wu, wd, g_fin, ts, 512, last)
        else:
            wg, wu, wd = w_gate_moe[i].astype(BF16), w_up_moe[i].astype(BF16), w_down_moe[i].astype(BF16)
            wr = jnp.concatenate([w_router[i], jnp.zeros((D_MODEL, LANES - N_EXPERTS), F32)], axis=1)
            h_p = _moe_routed(h1_p, w["g_ffn"], wr, wg, wu, wd, g_fin, last)
            h_s = _moe(hn_s, h1_s, _router(h1_s, w["g_ffn"], wr, ts)[0], wg, wu, wd, g_fin, ts, 512, last)

    y_prompt = h_p.reshape(batch, seq, D_MODEL)
    y_sample = h_s.reshape(nb, nq, D_MODEL)

    def stack(rows, idx, shape):
        return jnp.stack([r[idx].reshape(shape) for r in rows])

    return (y_prompt, y_sample,
            stack(rows_p, 0, (batch, seq, SB_HEADS, HEAD_DIM)), stack(rows_p, 1, (batch, seq, SB_HEADS, HEAD_DIM)),
            stack(rows_p, 2, (batch, seq, KV_RANK)), stack(rows_p, 3, (batch, seq, MLA_ROPE)),
            stack(rows_s, 0, (nb, nq, SB_HEADS, HEAD_DIM)), stack(rows_s, 1, (nb, nq, SB_HEADS, HEAD_DIM)),
            stack(rows_s, 2, (nb, nq, KV_RANK)), stack(rows_s, 3, (nb, nq, MLA_ROPE)))
```

```python
import functools

import jax
import jax.numpy as jnp
from jax import lax
from jax.experimental import pallas as pl
from jax.experimental.pallas import tpu as pltpu

F32 = jnp.float32
BF16 = jnp.bfloat16

D_MODEL = 1024
HEAD_DIM = 64
SB_HEADS = 8
SB_WIDTH = 512
MLA_HEADS = 8
MLA_NOPE = 64
MLA_ROPE = 32
MLA_V = 64
MLA_WIDTH = 512
Q_RANK = 256
KV_RANK = 128
ROPE_THETA = 10000.0
N_EXPERTS = 8
PAGE = 128
Q_BLOCK = 128
EPS = 1e-6
SB_SCALE = HEAD_DIM ** -0.5
MLA_SCALE = (MLA_NOPE + MLA_ROPE) ** -0.5
LOG2E = 1.4426950408889634

LANES = 128
QCAT = 2 * LANES
IN_COLS_PADDED = 3 * SB_WIDTH + Q_RANK + KV_RANK + 2 * LANES
VMEM_LIMIT = 52 * 1024 * 1024
PAGES_PER_STEP = 32
PAGE_GROUP = 8
MOE_CHUNKS = 7
MOE_ROWS = 64 * MOE_CHUNKS
NEG_INF = float("-inf")

_NT = (((1,), (1,)), ((), ()))


def _dot(a, b):
    return jnp.dot(a, b, preferred_element_type=F32)


def _dot_nt(a, b):
    return lax.dot_general(a, b, _NT, preferred_element_type=F32)


def _rms(x, g):
    return x * lax.rsqrt(jnp.mean(x * x, axis=-1, keepdims=True) + EPS) * g


def _params(sem):
    return pltpu.CompilerParams(dimension_semantics=sem, vmem_limit_bytes=VMEM_LIMIT)


def _full(shape):
    return pl.BlockSpec(shape, lambda *_: (0,) * len(shape))


def _proj_kernel(*refs, n_alias):
    (h_ref, g_ref, win_ref, gq_ref, wnope_ref, wra_ref, wrb_ref, gkv_ref, wuk_ref, cos_ref, sin_ref) = refs[:11]
    (q_ref, k_ref, v_ref, k16_ref, v16_ref, lat_ref, kr_ref, kvcat_ref, latt_ref, qcat_ref) = refs[11 + n_alias:]
    tm = h_ref.shape[0]
    xn = _rms(h_ref[...], g_ref[...]).astype(BF16)
    qkv = _dot(xn, win_ref[:, :3 * SB_WIDTH])
    q_ref[...] = (qkv[:, :SB_WIDTH] * SB_SCALE).astype(BF16)
    k = qkv[:, SB_WIDTH:2 * SB_WIDTH]
    v = qkv[:, 2 * SB_WIDTH:]
    k_ref[...] = k
    v_ref[...] = v
    k16_ref[...] = k.astype(BF16)
    v16_ref[...] = v.astype(BF16)

    rest = _dot(xn, win_ref[:, 3 * SB_WIDTH:])
    cos = cos_ref[...]
    sin = sin_ref[...]
    cqn = _rms(rest[:, :Q_RANK], gq_ref[...]).astype(BF16)
    lat = _rms(rest[:, Q_RANK:Q_RANK + KV_RANK], gkv_ref[...])
    o = Q_RANK + KV_RANK
    krope = rest[:, o:o + LANES] * cos + rest[:, o + LANES:] * sin
    lat_ref[...] = lat
    kr_ref[...] = krope[:, :MLA_ROPE]
    kvcat_ref[:, :LANES] = lat.astype(BF16)
    kvcat_ref[:, LANES:] = krope.astype(BF16)
    for i in range(tm // Q_BLOCK):
        latt_ref[i] = lat[i * Q_BLOCK:(i + 1) * Q_BLOCK].T.astype(BF16)

    qnope = _dot(cqn, wnope_ref[...]).astype(BF16)
    ra = _dot(cqn, wra_ref[...])
    rb = _dot(cqn, wrb_ref[...])
    for p in range(MLA_HEADS // 2):
        ql = _dot(qnope[:, p * LANES:(p + 1) * LANES], wuk_ref[p]).astype(BF16)
        for s in range(2):
            hd = 2 * p + s
            qr = (ra[:, hd * LANES:(hd + 1) * LANES] * cos + rb[:, hd * LANES:(hd + 1) * LANES] * sin).astype(BF16)
            for i in range(tm // Q_BLOCK):
                rows = slice(i * Q_BLOCK, (i + 1) * Q_BLOCK)
                qcat_ref[i, hd, :, :LANES] = ql[rows, s * LANES:(s + 1) * LANES]
                qcat_ref[i, hd, :, LANES:] = qr[rows]


def _proj(h, g, w, cos_t, sin_t, tm, layer=0, depth=1, stacked=()):
    t = h.shape[0]
    nt = t // tm
    row = lambda n: pl.BlockSpec((tm, n), lambda i: (i, 0))
    layer_row = pl.BlockSpec((tm, SB_WIDTH), lambda i: (layer * nt + i, 0))
    out_shape = (
        jax.ShapeDtypeStruct((t, SB_WIDTH), BF16),
        jax.ShapeDtypeStruct((depth * t, SB_WIDTH), F32),
        jax.ShapeDtypeStruct((depth * t, SB_WIDTH), F32),
        jax.ShapeDtypeStruct((t, SB_WIDTH), BF16),
        jax.ShapeDtypeStruct((t, SB_WIDTH), BF16),
        jax.ShapeDtypeStruct((t, KV_RANK), F32),
        jax.ShapeDtypeStruct((t, MLA_ROPE), F32),
        jax.ShapeDtypeStruct((t, QCAT), BF16),
        jax.ShapeDtypeStruct((t // Q_BLOCK, KV_RANK, Q_BLOCK), BF16),
        jax.ShapeDtypeStruct((t // Q_BLOCK, MLA_HEADS, Q_BLOCK, QCAT), BF16),
    )
    out_specs = (row(SB_WIDTH), layer_row, layer_row, row(SB_WIDTH), row(SB_WIDTH), row(KV_RANK),
                 row(MLA_ROPE), row(QCAT),
                 pl.BlockSpec((tm // Q_BLOCK, KV_RANK, Q_BLOCK), lambda i: (i, 0, 0)),
                 pl.BlockSpec((tm // Q_BLOCK, MLA_HEADS, Q_BLOCK, QCAT), lambda i: (i, 0, 0, 0)))
    in_specs = [row(D_MODEL), _full((1, D_MODEL)), _full((D_MODEL, IN_COLS_PADDED)), _full((1, Q_RANK)),
                _full((Q_RANK, MLA_HEADS * MLA_NOPE)), _full((Q_RANK, MLA_HEADS * LANES)),
                _full((Q_RANK, MLA_HEADS * LANES)), _full((1, KV_RANK)),
                _full((MLA_HEADS // 2, LANES, 2 * LANES)), row(LANES), row(LANES)]
    n_in = len(in_specs)
    in_specs += [pl.BlockSpec(memory_space=pl.ANY)] * len(stacked)
    return pl.pallas_call(
        functools.partial(_proj_kernel, n_alias=len(stacked)), grid=(nt,), in_specs=in_specs, out_specs=out_specs,
        out_shape=out_shape, input_output_aliases={n_in + a: 1 + a for a in range(len(stacked))},
        compiler_params=_params(("parallel",)), name="proj",
    )(h, g, w["w_in"], w["g_q"], w["w_nope"], w["w_rope_a"], w["w_rope_b"], w["g_kv"], w["w_uk_bd"], cos_t, sin_t,
      *stacked)


def _upper_ones(n):
    r = lax.broadcasted_iota(jnp.int32, (2 * n, 2 * n), 0) % n
    c = lax.broadcasted_iota(jnp.int32, (2 * n, 2 * n), 1)
    return jnp.where(jnp.logical_or(r > c, c >= n), 1.0, 0.0).astype(BF16)


def _sb_log_terms(z, upper_ones, visible):
    n = upper_ones.shape[0] // 2
    sp = jnp.log(1.0 + jnp.exp2(jnp.abs(z) * (-LOG2E)))
    log_beta = jnp.minimum(z, 0.0) - sp
    log_keep = log_beta - z
    if visible is not None:
        log_keep = jnp.where(visible, log_keep, 0.0)
    hi = log_keep.astype(BF16)
    lo = (log_keep - hi.astype(F32)).astype(BF16)
    later, total = [], None
    for t in reversed(range(z.shape[1] // n)):
        cols = slice(t * n, (t + 1) * n)
        sums = _dot(jnp.concatenate([hi[:, cols], lo[:, cols]], axis=1), upper_ones)
        later.insert(0, sums[:, :n] if total is None else sums[:, :n] + total)
        total = sums[:, n:] if total is None else total + sums[:, n:]
    return log_beta, later[0] if len(later) == 1 else jnp.concatenate(later, axis=1), total


def _sb_weights(log_beta, later, visible):
    a = jnp.exp(log_beta + later)
    if visible is not None:
        a = jnp.where(visible, a, 0.0)
    return a.astype(BF16)


def _sb_attn_kernel(q_ref, k_ref, v_ref, o_ref, qm_ref, acc_ref, carry_ref):
    qi = pl.program_id(1)
    pairs = SB_HEADS // 2
    rows = SB_HEADS * Q_BLOCK
    upper_ones = _upper_ones(LANES)
    left = lax.broadcasted_iota(jnp.int32, (Q_BLOCK, LANES), 1) < HEAD_DIM
    for p in range(pairs):
        q2 = q_ref[:, p * LANES:(p + 1) * LANES]
        zero = jnp.zeros_like(q2)
        qm_ref[p] = jnp.concatenate([jnp.where(left, q2, zero), jnp.where(left, zero, q2)], axis=0)

    groups = ((0, 1), (2, 3))
    grows = rows // len(groups)

    def block(kb, nblk, first):
        keys = pl.ds(pl.multiple_of(kb * Q_BLOCK, Q_BLOCK), nblk * Q_BLOCK)
        zs = [jnp.concatenate([_dot_nt(qm_ref[p], k_ref[keys, p * LANES:(p + 1) * LANES]) for p in g], axis=0)
              for g in groups]
        visible = None
        if first:
            r = lax.broadcasted_iota(jnp.int32, (grows, LANES), 0) % Q_BLOCK
            visible = lax.broadcasted_iota(jnp.int32, (grows, LANES), 1) < r
        terms = [_sb_log_terms(z, upper_ones, visible) for z in zs]
        for gi, g in enumerate(groups):
            log_beta, later, total = terms[gi]
            grp = slice(gi * grows, (gi + 1) * grows)
            if first:
                carry_ref[grp] = total
            else:
                carry = carry_ref[grp]
                later = later + jnp.concatenate([carry] * nblk, axis=1)
                carry_ref[grp] = carry + total
            a = _sb_weights(log_beta, later, visible)
            for i, p in enumerate(g):
                pv = _dot(a[i * 2 * Q_BLOCK:(i + 1) * 2 * Q_BLOCK], v_ref[keys, p * LANES:(p + 1) * LANES])
                acc_ref[p] = pv if first else acc_ref[p] + pv

    block(qi, 1, True)

    @pl.when(qi % 2 == 1)
    def _():
        block(qi - 1, 1, False)

    top = qi - qi % 2

    @pl.loop(0, qi // 2)
    def _(j):
        block(top - 2 - 2 * j, 2, False)

    for p in range(pairs):
        o_ref[:, p * LANES:(p + 1) * LANES] = jnp.where(left, acc_ref[p, :Q_BLOCK], acc_ref[p, Q_BLOCK:])


def _sb_attn(q, k16, v16, batch, seq):
    nq = seq // Q_BLOCK
    return pl.pallas_call(
        _sb_attn_kernel, grid=(batch, nq),
        in_specs=[pl.BlockSpec((Q_BLOCK, SB_WIDTH), lambda b, i: (b * nq + i, 0)),
                  pl.BlockSpec((seq, SB_WIDTH), lambda b, i: (b, 0)),
                  pl.BlockSpec((seq, SB_WIDTH), lambda b, i: (b, 0))],
        out_specs=pl.BlockSpec((Q_BLOCK, SB_WIDTH), lambda b, i: (b * nq + i, 0)),
        out_shape=jax.ShapeDtypeStruct((batch * seq, SB_WIDTH), F32),
        scratch_shapes=[pltpu.VMEM((SB_HEADS // 2, 2 * Q_BLOCK, LANES), BF16),
                        pltpu.VMEM((SB_HEADS // 2, 2 * Q_BLOCK, LANES), F32),
                        pltpu.VMEM((SB_HEADS * Q_BLOCK, LANES), F32)],
        compiler_params=_params(("parallel", "parallel")), name="sb_attn",
    )(q, k16, v16)


def _mla_attn_kernel(q_ref, kv_ref, latt_ref, o_ref, m_ref, l_ref, acc_ref):
    qi = pl.program_id(1)
    cols = MLA_HEADS * Q_BLOCK
    q = q_ref[0].reshape(cols, QCAT)
    key = lax.broadcasted_iota(jnp.int32, (Q_BLOCK, cols), 0)
    qry = lax.broadcasted_iota(jnp.int32, (Q_BLOCK, cols), 1) % Q_BLOCK

    def scores(start, size):
        kv = kv_ref[pl.ds(pl.multiple_of(start, Q_BLOCK), size), :]
        return _dot_nt(kv, q) * (MLA_SCALE * LOG2E)

    s = jnp.where(key <= qry, scores(qi * Q_BLOCK, Q_BLOCK), NEG_INF)
    m = jnp.max(s, axis=0, keepdims=True)
    p = jnp.exp2(s - m)
    m_ref[...] = m
    l_ref[...] = jnp.sum(p, axis=0, keepdims=True)
    acc_ref[...] = _dot(latt_ref[qi], p.astype(BF16))

    halves = (slice(0, cols // 2), slice(cols // 2, cols))

    def update(start, size, latt):
        kv = kv_ref[pl.ds(pl.multiple_of(start, Q_BLOCK), size), :]
        ss = [_dot_nt(kv, q[h]) * (MLA_SCALE * LOG2E) for h in halves]
        for h, s in zip(halves, ss):
            m_prev = m_ref[:, h]
            m_new = jnp.maximum(m_prev, jnp.max(s, axis=0, keepdims=True))
            alpha = jnp.exp2(m_prev - m_new)
            p = jnp.exp2(s - m_new)
            l_ref[:, h] = alpha * l_ref[:, h] + jnp.sum(p, axis=0, keepdims=True)
            acc_ref[:, h] = alpha * acc_ref[:, h] + _dot(latt, p.astype(BF16))
            m_ref[:, h] = m_new

    @pl.when(qi % 2 == 1)
    def _():
        update((qi - 1) * Q_BLOCK, Q_BLOCK, latt_ref[qi - 1])

    @pl.loop(0, qi // 2)
    def _(j):
        latt = jnp.concatenate([latt_ref[2 * j], latt_ref[2 * j + 1]], axis=1)
        update(j * 2 * Q_BLOCK, 2 * Q_BLOCK, latt)

    out = acc_ref[...] / l_ref[...]
    for hd in range(MLA_HEADS):
        blk = slice(hd * Q_BLOCK, (hd + 1) * Q_BLOCK)
        o_ref[:, hd * KV_RANK:(hd + 1) * KV_RANK] = out[:, blk].T.astype(BF16)


def _mla_attn(qcat, kvcat, latt, batch, seq):
    nq = seq // Q_BLOCK
    cols = MLA_HEADS * Q_BLOCK
    return pl.pallas_call(
        _mla_attn_kernel, grid=(batch, nq),
        in_specs=[pl.BlockSpec((1, MLA_HEADS, Q_BLOCK, QCAT), lambda b, i: (b * nq + i, 0, 0, 0)),
                  pl.BlockSpec((seq, QCAT), lambda b, i: (b, 0)),
                  pl.BlockSpec((nq, KV_RANK, Q_BLOCK), lambda b, i: (b, 0, 0))],
        out_specs=pl.BlockSpec((Q_BLOCK, MLA_HEADS * KV_RANK), lambda b, i: (b * nq + i, 0)),
        out_shape=jax.ShapeDtypeStruct((batch * seq, MLA_HEADS * KV_RANK), BF16),
        scratch_shapes=[pltpu.VMEM((1, cols), F32), pltpu.VMEM((1, cols), F32), pltpu.VMEM((KV_RANK, cols), F32)],
        compiler_params=_params(("parallel", "parallel")), name="mla_attn",
    )(qcat, kvcat, latt)


def _decode_kernel(pt_ref, qsb_ref, knew_ref, vnew_ref, qcat_ref, latnew_ref, krnew_ref, *refs):
    n = PAGES_PER_STEP
    kt_refs, vt_refs, lat_refs, krt_refs = refs[:n], refs[n:2 * n], refs[2 * n:3 * n], refs[3 * n:4 * n]
    osb_ref, omla_ref = refs[4 * n:4 * n + 2]
    acc_sb, carry_ref, m_ref, l_ref, acc_mla = refs[4 * n + 2:]
    step = pl.program_id(1)
    nq = qsb_ref.shape[0]
    rows = SB_HEADS * nq

    q_tiled = jnp.concatenate([qsb_ref[...]] * SB_HEADS, axis=0)
    row_head = lax.broadcasted_iota(jnp.int32, (rows, SB_WIDTH), 0) // nq
    lane_head = lax.broadcasted_iota(jnp.int32, (rows, SB_WIDTH), 1) // HEAD_DIM
    own_head = row_head == lane_head
    q_bd = jnp.where(own_head, q_tiled, 0.0).astype(BF16)
    qcat = qcat_ref[...].astype(BF16)
    q_lat = qcat[:, :KV_RANK]
    q_rope = qcat[:, KV_RANK:KV_RANK + MLA_ROPE]
    upper_ones = _upper_ones(PAGE)
    scale = MLA_SCALE * LOG2E

    @pl.when(step == 0)
    def _():
        pad = PAGE - nq
        q_idx = lax.broadcasted_iota(jnp.int32, (rows, PAGE), 0) % nq
        key_idx = lax.broadcasted_iota(jnp.int32, (rows, PAGE), 1)
        knew = jnp.concatenate([knew_ref[...], jnp.zeros((pad, SB_WIDTH), F32)], axis=0).astype(BF16)
        vnew = jnp.concatenate([vnew_ref[...], jnp.zeros((pad, SB_WIDTH), F32)], axis=0).astype(BF16)
        visible = key_idx < q_idx
        log_beta, later, total = _sb_log_terms(_dot_nt(q_bd, knew), upper_ones, visible)
        carry_ref[...] = total
        acc_sb[...] = _dot(_sb_weights(log_beta, later, visible), vnew)
        lat = jnp.concatenate([latnew_ref[...], jnp.zeros((pad, KV_RANK), F32)], axis=0).astype(BF16)
        kr = jnp.concatenate([krnew_ref[...], jnp.zeros((pad, MLA_ROPE), F32)], axis=0).astype(BF16)
        s = jnp.where(key_idx <= q_idx, (_dot_nt(q_lat, lat) + _dot_nt(q_rope, kr)) * scale, NEG_INF)
        m = jnp.max(s, axis=1, keepdims=True)
        p = jnp.exp2(s - m)
        m_ref[...] = m
        l_ref[...] = jnp.sum(p, axis=1, keepdims=True)
        acc_mla[...] = _dot(p.astype(BF16), lat)

    halves = [range(g, g + PAGE_GROUP) for g in range(0, n, PAGE_GROUP)]
    zs = [jnp.concatenate([_dot(q_bd, kt_refs[j][...].astype(BF16)) for j in half], axis=0) for half in halves]
    lats = [lat_refs[j][...].astype(BF16) for j in range(n)]
    s = jnp.concatenate([_dot_nt(q_lat, lats[j]) + _dot(q_rope, krt_refs[j][...].astype(BF16)) for j in range(n)],
                        axis=1) * scale
    terms = [_sb_log_terms(z, upper_ones, None) for z in zs]
    carry = carry_ref[...]
    acc = acc_sb[...]
    for half, (log_beta, later, total) in zip(halves, terms):
        for i, j in enumerate(half):
            page = slice(i * rows, (i + 1) * rows)
            a = _sb_weights(log_beta[page], later[page] + carry, None)
            carry = carry + total[page]
            acc = acc + _dot_nt(a, vt_refs[j][...].astype(BF16))
    carry_ref[...] = carry
    acc_sb[...] = acc

    m = m_ref[...]
    m_new = jnp.maximum(m, jnp.max(s, axis=1, keepdims=True))
    alpha = jnp.exp2(m - m_new)
    p = jnp.exp2(s - m_new)
    l = alpha * l_ref[...] + jnp.sum(p, axis=1, keepdims=True)
    p = p.astype(BF16)
    accm = alpha * acc_mla[...]
    for j in range(n):
        accm = accm + _dot(p[:, j * PAGE:(j + 1) * PAGE], lats[j])
    m_ref[...] = m_new
    l_ref[...] = l
    acc_mla[...] = accm

    @pl.when(step == pl.num_programs(1) - 1)
    def _():
        own = jnp.where(own_head, acc, 0.0)
        out = own[:nq]
        for hd in range(1, SB_HEADS):
            out = out + own[hd * nq:(hd + 1) * nq]
        osb_ref[...] = out
        omla_ref[...] = accm / l


def _decode(layer, page_table, qsb, knew, vnew, qcat, latnew, krnew, ckt, cvt, clat, ckrt):
    nb, nq, _ = qsb.shape
    n_pages = page_table.shape[1]
    n = PAGES_PER_STEP
    steps = n_pages // n
    rows = SB_HEADS * nq

    def per_seq(shape):
        return pl.BlockSpec((None,) + shape, lambda b, s, pt: (b, 0, 0))

    def page_spec(shape, j):
        return pl.BlockSpec((None, None) + shape,
                            lambda b, s, pt, j=j: (layer, pt[b, n_pages - 1 - (s * n + j)], 0, 0))

    in_specs = [per_seq((nq, SB_WIDTH)), per_seq((nq, SB_WIDTH)), per_seq((nq, SB_WIDTH)), per_seq((rows, QCAT)),
                per_seq((nq, KV_RANK)), per_seq((nq, MLA_ROPE))]
    for shape in ((SB_WIDTH, PAGE), (SB_WIDTH, PAGE), (PAGE, KV_RANK), (MLA_ROPE, PAGE)):
        in_specs += [page_spec(shape, j) for j in range(n)]
    grid_spec = pltpu.PrefetchScalarGridSpec(
        num_scalar_prefetch=1, grid=(nb, steps), in_specs=in_specs,
        out_specs=(per_seq((nq, SB_WIDTH)), per_seq((rows, KV_RANK))),
        scratch_shapes=[pltpu.VMEM((rows, SB_WIDTH), F32), pltpu.VMEM((rows, PAGE), F32), pltpu.VMEM((rows, 1), F32),
                        pltpu.VMEM((rows, 1), F32), pltpu.VMEM((rows, KV_RANK), F32)])
    return pl.pallas_call(
        _decode_kernel, grid_spec=grid_spec,
        out_shape=(jax.ShapeDtypeStruct((nb, nq, SB_WIDTH), F32), jax.ShapeDtypeStruct((nb, rows, KV_RANK), F32)),
        compiler_params=_params(("parallel", "arbitrary")), name="decode_attn",
    )(page_table, qsb, knew, vnew, qcat, latnew, krnew, *([ckt] * n), *([cvt] * n), *([clat] * n), *([ckrt] * n))


def _post_kernel(osb_ref, omla_ref, h_ref, gsb_ref, gmla_ref, wuv_ref, wout_ref, gffn_ref, h1_ref, hn_ref):
    pairs = [_dot(omla_ref[:, p * 2 * KV_RANK:(p + 1) * 2 * KV_RANK], wuv_ref[p]) for p in range(MLA_HEADS // 2)]
    o_mla = jnp.concatenate(pairs, axis=1)
    mixed = jnp.concatenate([_rms(osb_ref[...], gsb_ref[...]), _rms(o_mla, gmla_ref[...])], axis=1).astype(BF16)
    h1 = h_ref[...] + _dot(mixed, wout_ref[...])
    h1_ref[...] = h1
    hn_ref[...] = _rms(h1, gffn_ref[...]).astype(BF16)


def _post(osb, omla, h, w, tm):
    t = h.shape[0]
    row = lambda n: pl.BlockSpec((tm, n), lambda i: (i, 0))
    return pl.pallas_call(
        _post_kernel, grid=(t // tm,),
        in_specs=[row(SB_WIDTH), row(MLA_HEADS * KV_RANK), row(D_MODEL), _full((1, SB_WIDTH)), _full((1, MLA_WIDTH)),
                  _full((MLA_HEADS // 2, 2 * KV_RANK, 2 * MLA_V)), _full((D_MODEL, D_MODEL)), _full((1, D_MODEL))],
        out_specs=(row(D_MODEL), row(D_MODEL)),
        out_shape=(jax.ShapeDtypeStruct((t, D_MODEL), F32), jax.ShapeDtypeStruct((t, D_MODEL), BF16)),
        compiler_params=_params(("parallel",)), name="post_attn",
    )(osb, omla, h, w["g_out_sb"], w["g_out_mla"], w["w_uv_bd"], w["w_out"], w["g_ffn"])


def _swiglu_chunk(x, wg, wu, wd):
    g = _dot(x, wg)
    act = (g * jax.nn.sigmoid(g) * _dot(x, wu)).astype(BF16)
    return _dot(act, wd)


def _ffn_kernel(hn_ref, h1_ref, wg_ref, wu_ref, wd_ref, gfin_ref, o_ref, acc_ref, *, final_norm):
    j = pl.program_id(1)

    @pl.when(j == 0)
    def _():
        acc_ref[...] = jnp.zeros_like(acc_ref)

    acc_ref[...] += _swiglu_chunk(hn_ref[...], wg_ref[...], wu_ref[...], wd_ref[...])

    @pl.when(j == pl.num_programs(1) - 1)
    def _():
        out = h1_ref[...] + acc_ref[...]
        o_ref[...] = _rms(out, gfin_ref[...]) if final_norm else out


def _ffn(hn, h1, wg, wu, wd, g_final, tm, tf, final_norm):
    t = hn.shape[0]
    f = wg.shape[1]
    return pl.pallas_call(
        functools.partial(_ffn_kernel, final_norm=final_norm), grid=(t // tm, f // tf),
        in_specs=[pl.BlockSpec((tm, D_MODEL), lambda i, j: (i, 0)), pl.BlockSpec((tm, D_MODEL), lambda i, j: (i, 0)),
                  pl.BlockSpec((D_MODEL, tf), lambda i, j: (0, j)), pl.BlockSpec((D_MODEL, tf), lambda i, j: (0, j)),
                  pl.BlockSpec((tf, D_MODEL), lambda i, j: (j, 0)), _full((1, D_MODEL))],
        out_specs=pl.BlockSpec((tm, D_MODEL), lambda i, j: (i, 0)),
        out_shape=jax.ShapeDtypeStruct((t, D_MODEL), F32),
        scratch_shapes=[pltpu.VMEM((tm, D_MODEL), F32)],
        compiler_params=_params(("parallel", "arbitrary")), name="ffn_dense",
    )(hn, h1, wg, wu, wd, g_final)


def _router_kernel(h1_ref, gffn_ref, wr_ref, comb_ref, meta_ref):
    hn = _rms(h1_ref[...], gffn_ref[...])
    w = wr_ref[...]
    hn_hi, w_hi = hn.astype(BF16), w.astype(BF16)
    hn_lo, w_lo = (hn - hn_hi.astype(F32)).astype(BF16), (w - w_hi.astype(F32)).astype(BF16)
    logits = _dot(hn_hi, w_hi) + _dot(hn_lo, w_hi) + _dot(hn_hi, w_lo)
    lane = lax.broadcasted_iota(jnp.int32, logits.shape, 1)
    logits = jnp.where(lane < N_EXPERTS, logits, NEG_INF)
    m1 = jnp.max(logits, axis=1, keepdims=True)
    i1 = jnp.min(jnp.where(logits == m1, lane, LANES), axis=1, keepdims=True)
    rest = jnp.where(lane == i1, NEG_INF, logits)
    m2 = jnp.max(rest, axis=1, keepdims=True)
    i2 = jnp.min(jnp.where(rest == m2, lane, LANES), axis=1, keepdims=True)
    e2 = jnp.exp(m2 - m1)
    g1 = 1.0 / (1.0 + e2)
    g2 = e2 / (1.0 + e2)
    comb_ref[...] = jnp.where(lane == i1, g1, 0.0) + jnp.where(lane == i2, g2, 0.0)
    meta_ref[...] = jnp.where(lane == 0, i1.astype(F32), jnp.where(lane == 1, i2.astype(F32),
                              jnp.where(lane == 2, g1, jnp.where(lane == 3, g2, 0.0))))


def _router(h1, g_ffn, w_router, tm):
    t = h1.shape[0]
    return pl.pallas_call(
        _router_kernel, grid=(t // tm,),
        in_specs=[pl.BlockSpec((tm, D_MODEL), lambda i: (i, 0)), _full((1, D_MODEL)), _full((D_MODEL, LANES))],
        out_specs=(pl.BlockSpec((tm, LANES), lambda i: (i, 0)), pl.BlockSpec((tm, LANES), lambda i: (i, 0))),
        out_shape=(jax.ShapeDtypeStruct((t, LANES), F32), jax.ShapeDtypeStruct((t, LANES), F32)),
        compiler_params=_params(("parallel",)), name="router",
    )(h1, g_ffn, w_router)


def _moe_kernel(hn_ref, h1_ref, comb_ref, wg_ref, wu_ref, wd_ref, gfin_ref, o_ref, acc_ref, *, final_norm):
    e = pl.program_id(1)
    j = pl.program_id(2)

    @pl.when(jnp.logical_and(e == 0, j == 0))
    def _():
        acc_ref[...] = jnp.zeros_like(acc_ref)

    lane = lax.broadcasted_iota(jnp.int32, comb_ref.shape, 1)
    gate = jnp.sum(jnp.where(lane == e, comb_ref[...], 0.0), axis=1, keepdims=True)
    acc_ref[...] += gate * _swiglu_chunk(hn_ref[...], wg_ref[...], wu_ref[...], wd_ref[...])

    @pl.when(jnp.logical_and(e == pl.num_programs(1) - 1, j == pl.num_programs(2) - 1))
    def _():
        out = h1_ref[...] + acc_ref[...]
        o_ref[...] = _rms(out, gfin_ref[...]) if final_norm else out


def _moe(hn, h1, comb, wg, wu, wd, g_final, tm, tf, final_norm):
    t = hn.shape[0]
    ne, _, f = wg.shape
    tok = lambda n: pl.BlockSpec((tm, n), lambda i, e, j: (i, 0))
    return pl.pallas_call(
        functools.partial(_moe_kernel, final_norm=final_norm), grid=(t // tm, ne, f // tf),
        in_specs=[tok(D_MODEL), tok(D_MODEL), tok(LANES),
                  pl.BlockSpec((None, D_MODEL, tf), lambda i, e, j: (e, 0, j)),
                  pl.BlockSpec((None, D_MODEL, tf), lambda i, e, j: (e, 0, j)),
                  pl.BlockSpec((None, tf, D_MODEL), lambda i, e, j: (e, j, 0)), _full((1, D_MODEL))],
        out_specs=tok(D_MODEL),
        out_shape=jax.ShapeDtypeStruct((t, D_MODEL), F32),
        scratch_shapes=[pltpu.VMEM((tm, D_MODEL), F32)],
        compiler_params=_params(("parallel", "arbitrary", "arbitrary")), name="ffn_moe",
    )(hn, h1, comb, wg, wu, wd, g_final)


def _routing_tables(e1, e2, n_tiles):
    t = e1.shape[0]
    tr = MOE_ROWS
    flat = jnp.stack([e1, e2], axis=1).reshape(2 * t)
    order = jnp.argsort(flat, stable=True).astype(jnp.int32)
    counts = jnp.sum(flat[:, None] == jnp.arange(N_EXPERTS, dtype=jnp.int32)[None, :], axis=0).astype(jnp.int32)
    padded = (counts + tr - 1) // tr * tr
    pend = jnp.cumsum(padded)
    pstart = pend - padded
    ustart = jnp.cumsum(counts) - counts
    n_valid = (pend[-1] // tr).astype(jnp.int32)
    tile_start = jnp.arange(n_tiles, dtype=jnp.int32) * tr
    tile_expert = jnp.minimum(jnp.sum(tile_start[:, None] >= pend[None, :], axis=1), N_EXPERTS - 1).astype(jnp.int32)
    slot = jnp.arange(n_tiles * tr, dtype=jnp.int32)
    e = jnp.repeat(tile_expert, tr)
    rank = slot - pstart[e]
    valid = jnp.logical_and(slot < pend[-1], rank < counts[e])
    a = order[jnp.clip(ustart[e] + rank, 0, 2 * t - 1)]
    src = jnp.where(valid, a // 2, 0)
    dst = jnp.where(valid, (a % 2) * t + a // 2, 2 * t + e * tr + jnp.clip(rank - counts[e], 0, tr - 1))
    dst = jnp.concatenate([2 * t + N_EXPERTS * tr + jnp.arange(tr, dtype=jnp.int32), dst])
    return src.reshape(n_tiles, 1, tr), dst.reshape(n_tiles + 1, 1, tr), tile_expert, n_valid.reshape(1)


def _moe_grouped_kernel(te_ref, nv_ref, src_cur, src_next, dst_prev, dst_cur, h1_hbm, gffn_ref, wg_ref, wu_ref, wd_ref,
                        y_hbm, xbuf, xn_ref, acc_ref, obuf, gsem, ssem):
    i = pl.program_id(0)
    j = pl.program_id(1)
    nj = pl.num_programs(1)
    nv = nv_ref[0]
    tr = xn_ref.shape[0]
    ch = tr // MOE_CHUNKS
    slot = i % 2
    other = 1 - slot

    def gather_row(tok_ref, r, s):
        return pltpu.make_async_copy(h1_hbm.at[pl.ds(tok_ref[0, r], 1)], xbuf.at[s, pl.ds(r, 1)], gsem.at[s])

    def scatter_row(row_ref, r, s):
        return pltpu.make_async_copy(obuf.at[s, pl.ds(r, 1)], y_hbm.at[pl.ds(row_ref[0, r], 1)], ssem.at[s])

    @pl.when(i < nv)
    def _():
        @pl.when(jnp.logical_and(i == 0, j == 0))
        def _():
            obuf[...] = jnp.zeros_like(obuf)
            for r in range(tr):
                gather_row(src_cur, r, 0).start()
            for r in range(tr):
                gather_row(src_cur, r, 0).wait()

        @pl.when(j == 0)
        def _():
            xn_ref[...] = _rms(xbuf[slot], gffn_ref[...]).astype(BF16)
            acc_ref[...] = jnp.zeros_like(acc_ref)

        for c in range(ch):
            gather_row(src_next, j * ch + c, other).start()
        for c in range(ch):
            scatter_row(dst_prev, j * ch + c, other).start()

        acc_ref[...] += _swiglu_chunk(xn_ref[...], wg_ref[...], wu_ref[...], wd_ref[...])

        @pl.when(j == nj - 1)
        def _():
            obuf[slot] = acc_ref[...]
            for r in range(tr):
                gather_row(src_next, r, other).wait()
            for r in range(tr):
                scatter_row(dst_prev, r, other).wait()

            @pl.when(i == nv - 1)
            def _():
                for r in range(tr):
                    scatter_row(dst_cur, r, slot).start()
                for r in range(tr):
                    scatter_row(dst_cur, r, slot).wait()


def _moe_grouped(h1, g_ffn, src, dst, tile_expert, n_valid, wg, wu, wd):
    t = h1.shape[0]
    n_tiles, _, tr = src.shape
    ne, _, f = wg.shape
    tf = f // MOE_CHUNKS
    last = n_tiles - 1
    smem = functools.partial(pl.BlockSpec, memory_space=pltpu.SMEM)
    in_specs = [
        smem((None, 1, tr), lambda i, j, te, nv: (i, 0, 0)),
        smem((None, 1, tr), lambda i, j, te, nv: (jnp.minimum(i + 1, last), 0, 0)),
        smem((None, 1, tr), lambda i, j, te, nv: (i, 0, 0)),
        smem((None, 1, tr), lambda i, j, te, nv: (i + 1, 0, 0)),
        pl.BlockSpec(memory_space=pl.ANY),
        pl.BlockSpec((1, D_MODEL), lambda i, j, te, nv: (0, 0)),
        pl.BlockSpec((None, D_MODEL, tf), lambda i, j, te, nv: (te[i], 0, j)),
        pl.BlockSpec((None, D_MODEL, tf), lambda i, j, te, nv: (te[i], 0, j)),
        pl.BlockSpec((None, tf, D_MODEL), lambda i, j, te, nv: (te[i], j, 0)),
    ]
    grid_spec = pltpu.PrefetchScalarGridSpec(
        num_scalar_prefetch=2, grid=(n_tiles, MOE_CHUNKS), in_specs=in_specs,
        out_specs=pl.BlockSpec(memory_space=pl.ANY),
        scratch_shapes=[pltpu.VMEM((2, tr, D_MODEL), F32), pltpu.VMEM((tr, D_MODEL), BF16),
                        pltpu.VMEM((tr, D_MODEL), F32), pltpu.VMEM((2, tr, D_MODEL), F32),
                        pltpu.SemaphoreType.DMA((2,)), pltpu.SemaphoreType.DMA((2,))])
    return pl.pallas_call(
        _moe_grouped_kernel, grid_spec=grid_spec,
        out_shape=jax.ShapeDtypeStruct((2 * t + (ne + 1) * tr, D_MODEL), F32),
        compiler_params=_params(("arbitrary", "arbitrary")), name="ffn_moe_grouped",
    )(tile_expert, n_valid, src, src, dst, dst, h1, g_ffn, wg, wu, wd)


def _moe_combine_kernel(h1_ref, y1_ref, y2_ref, meta_ref, gfin_ref, o_ref, *, final_norm):
    meta = meta_ref[...]
    lane = lax.broadcasted_iota(jnp.int32, meta.shape, 1)
    g1 = jnp.sum(jnp.where(lane == 2, meta, 0.0), axis=1, keepdims=True)
    g2 = jnp.sum(jnp.where(lane == 3, meta, 0.0), axis=1, keepdims=True)
    out = h1_ref[...] + g1 * y1_ref[...] + g2 * y2_ref[...]
    o_ref[...] = _rms(out, gfin_ref[...]) if final_norm else out


def _moe_combine(h1, y, meta, g_final, tm, final_norm):
    t = h1.shape[0]
    nt = t // tm
    return pl.pallas_call(
        functools.partial(_moe_combine_kernel, final_norm=final_norm), grid=(nt,),
        in_specs=[pl.BlockSpec((tm, D_MODEL), lambda i: (i, 0)), pl.BlockSpec((tm, D_MODEL), lambda i: (i, 0)),
                  pl.BlockSpec((tm, D_MODEL), lambda i: (i + nt, 0)), pl.BlockSpec((tm, LANES), lambda i: (i, 0)),
                  _full((1, D_MODEL))],
        out_specs=pl.BlockSpec((tm, D_MODEL), lambda i: (i, 0)),
        out_shape=jax.ShapeDtypeStruct((t, D_MODEL), F32),
        compiler_params=_params(("parallel",)), name="moe_combine",
    )(h1, y, y, meta, g_final)


def _moe_routed(h1, g_ffn, w_router, wg, wu, wd, g_final, final_norm):
    t = h1.shape[0]
    _, meta = _router(h1, g_ffn, w_router, 1024)
    e1 = meta[:, 0].astype(jnp.int32)
    e2 = meta[:, 1].astype(jnp.int32)
    n_tiles = -(-2 * t // MOE_ROWS) + N_EXPERTS
    src, dst, tile_expert, n_valid = _routing_tables(e1, e2, n_tiles)
    y = _moe_grouped(h1, g_ffn, src, dst, tile_expert, n_valid, wg, wu, wd)
    return _moe_combine(h1, y, meta, g_final, 512, final_norm)


def _prep_layer(l, g_attn_norm, w_in, g_q_norm, w_uq, g_kv_norm, w_uk, w_uv, g_out_sb, g_out_mla, w_out, g_ffn_norm):
    half = MLA_ROPE // 2
    o = 3 * SB_WIDTH + Q_RANK + KV_RANK
    kr = w_in[l][:, o:]
    pad = jnp.zeros((D_MODEL, LANES - MLA_ROPE), F32)
    kr_a = jnp.concatenate([kr, pad], axis=1)
    kr_b = jnp.concatenate([-kr[:, half:], kr[:, :half], pad], axis=1)
    w_in_p = jnp.concatenate([w_in[l][:, :o], kr_a, kr_b], axis=1).astype(BF16)

    uq = w_uq[l]
    w_nope = uq[:, :, :MLA_NOPE].reshape(Q_RANK, MLA_HEADS * MLA_NOPE).astype(BF16)
    qr = uq[:, :, MLA_NOPE:]
    hpad = jnp.zeros((Q_RANK, MLA_HEADS, LANES - MLA_ROPE), F32)
    w_rope_a = jnp.concatenate([qr, hpad], axis=2).reshape(Q_RANK, MLA_HEADS * LANES).astype(BF16)
    w_rope_b = jnp.concatenate([-qr[:, :, half:], qr[:, :, :half], hpad], axis=2)
    w_rope_b = w_rope_b.reshape(Q_RANK, MLA_HEADS * LANES).astype(BF16)

    uk = jnp.transpose(w_uk[l], (1, 2, 0)).reshape(MLA_HEADS // 2, 2, MLA_NOPE, KV_RANK)
    zk = jnp.zeros((MLA_HEADS // 2, MLA_NOPE, KV_RANK), F32)
    w_uk_bd = jnp.concatenate([jnp.concatenate([uk[:, 0], zk], axis=2),
                               jnp.concatenate([zk, uk[:, 1]], axis=2)], axis=1).astype(BF16)
    uv = jnp.transpose(w_uv[l], (1, 0, 2)).reshape(MLA_HEADS // 2, 2, KV_RANK, MLA_V)
    zv = jnp.zeros((MLA_HEADS // 2, KV_RANK, MLA_V), F32)
    w_uv_bd = jnp.concatenate([jnp.concatenate([uv[:, 0], zv], axis=2),
                               jnp.concatenate([zv, uv[:, 1]], axis=2)], axis=1).astype(BF16)
    return {
        "g_attn": g_attn_norm[l][None], "w_in": w_in_p, "g_q": g_q_norm[l][None], "w_nope": w_nope,
        "w_rope_a": w_rope_a, "w_rope_b": w_rope_b, "g_kv": g_kv_norm[l][None], "w_uk_bd": w_uk_bd,
        "w_uv_bd": w_uv_bd, "g_out_sb": g_out_sb[l][None], "g_out_mla": g_out_mla[l][None],
        "w_out": w_out[l].astype(BF16), "g_ffn": g_ffn_norm[l][None],
    }


def _rope_tables(pos):
    inv_freq = 1.0 / (ROPE_THETA ** (jnp.arange(0, MLA_ROPE, 2, dtype=F32) / MLA_ROPE))
    ang = pos.astype(F32)[:, None] * inv_freq[None, :]
    pad = jnp.zeros((pos.shape[0], LANES - MLA_ROPE), F32)
    cos, sin = jnp.cos(ang), jnp.sin(ang)
    return jnp.concatenate([cos, cos, pad], axis=1), jnp.concatenate([sin, sin, pad], axis=1)


def kernel(x_prompt, x_sample, cache_sb_k, cache_sb_v, cache_mla_latent, cache_mla_krope, page_table, g_attn_norm,
           w_in, g_q_norm, w_uq, g_kv_norm, w_uk, w_uv, g_out_sb, g_out_mla, w_out, g_ffn_norm, w_gate_dense,
           w_up_dense, w_down_dense, w_router, w_gate_moe, w_up_moe, w_down_moe, g_final):
    batch, seq, _ = x_prompt.shape
    nb, nq, _ = x_sample.shape
    depth, n_pool = cache_sb_k.shape[:2]
    past_len = page_table.shape[1] * PAGE
    tp, ts = batch * seq, nb * nq

    ckt = jnp.transpose(cache_sb_k, (0, 1, 3, 4, 2)).reshape(depth, n_pool, SB_WIDTH, PAGE)
    cvt = jnp.transpose(cache_sb_v, (0, 1, 3, 4, 2)).reshape(depth, n_pool, SB_WIDTH, PAGE)
    ckrt = jnp.transpose(cache_mla_krope, (0, 1, 3, 2))
    cos_p, sin_p = _rope_tables(jnp.tile(jnp.arange(seq, dtype=jnp.int32), batch))
    cos_s, sin_s = _rope_tables(jnp.tile(past_len + jnp.arange(nq, dtype=jnp.int32), nb))
    g_fin = g_final[None]

    h_p = x_prompt.reshape(tp, D_MODEL)
    h_s = x_sample.reshape(ts, D_MODEL)
    rows_p, rows_s = [], []
    kv_stacked = ()
    for l in range(depth):
        w = _prep_layer(l, g_attn_norm, w_in, g_q_norm, w_uq, g_kv_norm, w_uk, w_uv, g_out_sb, g_out_mla, w_out,
                        g_ffn_norm)
        last = l == depth - 1

        q, k_all, v_all, k16, v16, lat, kr, kvcat, latt, qcat = _proj(h_p, w["g_attn"], w, cos_p, sin_p, 512, l, depth,
                                                                      kv_stacked)
        kv_stacked = (k_all, v_all)
        rows_p.append((lat, kr))
        o_sb = _sb_attn(q, k16, v16, batch, seq)
        o_mla = _mla_attn(qcat, kvcat, latt, batch, seq)
        h1_p, hn_p = _post(o_sb, o_mla, h_p, w, 512)

        q, k, v, _, _, lat, kr, _, _, qcat = _proj(h_s, w["g_attn"], w, cos_s, sin_s, ts)
        rows_s.append((k, v, lat, kr))
        qcat_s = qcat.reshape(ts // Q_BLOCK, MLA_HEADS, Q_BLOCK // nq, nq, QCAT)
        qcat_s = jnp.transpose(qcat_s, (0, 2, 1, 3, 4)).reshape(nb, MLA_HEADS * nq, QCAT).astype(F32)
        o_sb, o_mla = _decode(l, page_table, q.astype(F32).reshape(nb, nq, SB_WIDTH), k.reshape(nb, nq, SB_WIDTH),
                              v.reshape(nb, nq, SB_WIDTH), qcat_s, lat.reshape(nb, nq, KV_RANK),
                              kr.reshape(nb, nq, MLA_ROPE), ckt, cvt, cache_mla_latent, ckrt)
        o_mla = jnp.transpose(o_mla.reshape(nb, MLA_HEADS, nq, KV_RANK), (0, 2, 1, 3))
        o_mla = o_mla.reshape(ts, MLA_HEADS * KV_RANK).astype(BF16)
        h1_s, hn_s = _post(o_sb.reshape(ts, SB_WIDTH), o_mla, h_s, w, ts)

        i = l // 2
        if l % 2 == 0:
            wg, wu, wd = w_gate_dense[i].astype(BF16), w_up_dense[i].astype(BF16), w_down_dense[i].astype(BF16)
            h_p = _ffn(hn_p, h1_p, wg, wu, wd, g_fin, 1024, 512, last)
            h_s = _ffn(hn_s, h1_s, wg, wu, wd, g_fin, ts, 512, last)
        else:
            wg, wu, wd = w_gate_moe[i].astype(BF16), w_up_moe[i].astype(BF16), w_down_moe[i].astype(BF16)
            wr = jnp.concatenate([w_router[i], jnp.zeros((D_MODEL, LANES - N_EXPERTS), F32)], axis=1)
            h_p = _moe_routed(h1_p, w["g_ffn"], wr, wg, wu, wd, g_fin, last)
            h_s = _moe(hn_s, h1_s, _router(h1_s, w["g_ffn"], wr, ts)[0], wg, wu, wd, g_fin, ts, 1792, last)

    y_prompt = h_p.reshape(batch, seq, D_MODEL)
    y_sample = h_s.reshape(nb, nq, D_MODEL)

    def stack(rows, idx, shape):
        return jnp.stack([r[idx].reshape(shape) for r in rows])

    return (y_prompt, y_sample,
            kv_stacked[0].reshape(depth, batch, seq, SB_HEADS, HEAD_DIM),
            kv_stacked[1].reshape(depth, batch, seq, SB_HEADS, HEAD_DIM),
            stack(rows_p, 0, (batch, seq, KV_RANK)), stack(rows_p, 1, (batch, seq, MLA_ROPE)),
            stack(rows_s, 0, (nb, nq, SB_HEADS, HEAD_DIM)), stack(rows_s, 1, (nb, nq, SB_HEADS, HEAD_DIM)),
            stack(rows_s, 2, (nb, nq, KV_RANK)), stack(rows_s, 3, (nb, nq, MLA_ROPE)))
```

```python
import functools

import jax
import jax.numpy as jnp
from jax import lax
from jax.experimental import pallas as pl
from jax.experimental.pallas import tpu as pltpu

F32 = jnp.float32
BF16 = jnp.bfloat16

D_MODEL = 1024
HEAD_DIM = 64
SB_HEADS = 8
SB_WIDTH = 512
MLA_HEADS = 8
MLA_NOPE = 64
MLA_ROPE = 32
MLA_V = 64
MLA_WIDTH = 512
Q_RANK = 256
KV_RANK = 128
ROPE_THETA = 10000.0
N_EXPERTS = 8
PAGE = 128
Q_BLOCK = 128
EPS = 1e-6
SB_SCALE = HEAD_DIM ** -0.5
MLA_SCALE = (MLA_NOPE + MLA_ROPE) ** -0.5
LOG2E = 1.4426950408889634

LANES = 128
QCAT = 2 * LANES
IN_COLS_PADDED = 3 * SB_WIDTH + Q_RANK + KV_RANK + 2 * LANES
VMEM_LIMIT = 52 * 1024 * 1024
PAGES_PER_STEP = 32
PAGE_GROUP = 8
MOE_CHUNKS = 7
MOE_ROWS = 64 * MOE_CHUNKS
NEG_INF = float("-inf")

_NT = (((1,), (1,)), ((), ()))


def _dot(a, b):
    return jnp.dot(a, b, preferred_element_type=F32)


def _dot_nt(a, b):
    return lax.dot_general(a, b, _NT, preferred_element_type=F32)


def _rms(x, g):
    return x * lax.rsqrt(jnp.mean(x * x, axis=-1, keepdims=True) + EPS) * g


def _params(sem):
    return pltpu.CompilerParams(dimension_semantics=sem, vmem_limit_bytes=VMEM_LIMIT)


def _full(shape):
    return pl.BlockSpec(shape, lambda *_: (0,) * len(shape))


def _proj_kernel(*refs, n_alias, key_minor):
    (h_ref, g_ref, win_ref, gq_ref, wnope_ref, wra_ref, wrb_ref, gkv_ref, wuk_ref, cos_ref, sin_ref) = refs[:11]
    (q_ref, k_ref, v_ref, k16_ref, v16_ref, lat_ref, kr_ref, kvcat_ref, latt_ref, qcat_ref) = refs[11 + n_alias:]
    tm = h_ref.shape[0]
    xn = _rms(h_ref[...], g_ref[...]).astype(BF16)
    qkv = _dot(xn, win_ref[:, :3 * SB_WIDTH])
    q_ref[...] = (qkv[:, :SB_WIDTH] * SB_SCALE).astype(BF16)
    k = qkv[:, SB_WIDTH:2 * SB_WIDTH]
    v = qkv[:, 2 * SB_WIDTH:]
    k_ref[...] = k.T if key_minor else k
    v_ref[...] = v.T if key_minor else v
    k16_ref[...] = k.astype(BF16)
    v16_ref[...] = v.astype(BF16)

    rest = _dot(xn, win_ref[:, 3 * SB_WIDTH:])
    cos = cos_ref[...]
    sin = sin_ref[...]
    cqn = _rms(rest[:, :Q_RANK], gq_ref[...]).astype(BF16)
    lat = _rms(rest[:, Q_RANK:Q_RANK + KV_RANK], gkv_ref[...])
    o = Q_RANK + KV_RANK
    krope = rest[:, o:o + LANES] * cos + rest[:, o + LANES:] * sin
    lat_ref[...] = lat
    kr_ref[...] = krope[:, :MLA_ROPE]
    kvcat_ref[:, :LANES] = lat.astype(BF16)
    kvcat_ref[:, LANES:] = krope.astype(BF16)
    for i in range(tm // Q_BLOCK):
        latt_ref[i] = lat[i * Q_BLOCK:(i + 1) * Q_BLOCK].T.astype(BF16)

    qnope = _dot(cqn, wnope_ref[...]).astype(BF16)
    ra = _dot(cqn, wra_ref[...])
    rb = _dot(cqn, wrb_ref[...])
    for p in range(MLA_HEADS // 2):
        ql = _dot(qnope[:, p * LANES:(p + 1) * LANES], wuk_ref[p]).astype(BF16)
        for s in range(2):
            hd = 2 * p + s
            qr = (ra[:, hd * LANES:(hd + 1) * LANES] * cos + rb[:, hd * LANES:(hd + 1) * LANES] * sin).astype(BF16)
            for i in range(tm // Q_BLOCK):
                rows = slice(i * Q_BLOCK, (i + 1) * Q_BLOCK)
                qcat_ref[i, hd, :, :LANES] = ql[rows, s * LANES:(s + 1) * LANES]
                qcat_ref[i, hd, :, LANES:] = qr[rows]


def _proj(h, g, w, cos_t, sin_t, tm, layer=0, depth=1, stacked=(), seq=None):
    t = h.shape[0]
    nt = t // tm
    row = lambda n: pl.BlockSpec((tm, n), lambda i: (i, 0))
    if seq is None:
        kv_shape = jax.ShapeDtypeStruct((depth * t, SB_WIDTH), F32)
        layer_row = pl.BlockSpec((tm, SB_WIDTH), lambda i: (layer * nt + i, 0))
    else:
        per_seq = seq // tm
        kv_shape = jax.ShapeDtypeStruct((depth * (t // seq), SB_WIDTH, seq), F32)
        layer_row = pl.BlockSpec((None, SB_WIDTH, tm), lambda i: (layer * (t // seq) + i // per_seq, 0, i % per_seq))
    out_shape = (
        jax.ShapeDtypeStruct((t, SB_WIDTH), BF16),
        kv_shape,
        kv_shape,
        jax.ShapeDtypeStruct((t, SB_WIDTH), BF16),
        jax.ShapeDtypeStruct((t, SB_WIDTH), BF16),
        jax.ShapeDtypeStruct((t, KV_RANK), F32),
        jax.ShapeDtypeStruct((t, MLA_ROPE), F32),
        jax.ShapeDtypeStruct((t, QCAT), BF16),
        jax.ShapeDtypeStruct((t // Q_BLOCK, KV_RANK, Q_BLOCK), BF16),
        jax.ShapeDtypeStruct((t // Q_BLOCK, MLA_HEADS, Q_BLOCK, QCAT), BF16),
    )
    out_specs = (row(SB_WIDTH), layer_row, layer_row, row(SB_WIDTH), row(SB_WIDTH), row(KV_RANK),
                 row(MLA_ROPE), row(QCAT),
                 pl.BlockSpec((tm // Q_BLOCK, KV_RANK, Q_BLOCK), lambda i: (i, 0, 0)),
                 pl.BlockSpec((tm // Q_BLOCK, MLA_HEADS, Q_BLOCK, QCAT), lambda i: (i, 0, 0, 0)))
    in_specs = [row(D_MODEL), _full((1, D_MODEL)), _full((D_MODEL, IN_COLS_PADDED)), _full((1, Q_RANK)),
                _full((Q_RANK, MLA_HEADS * MLA_NOPE)), _full((Q_RANK, MLA_HEADS * LANES)),
                _full((Q_RANK, MLA_HEADS * LANES)), _full((1, KV_RANK)),
                _full((MLA_HEADS // 2, LANES, 2 * LANES)), row(LANES), row(LANES)]
    n_in = len(in_specs)
    in_specs += [pl.BlockSpec(memory_space=pl.ANY)] * len(stacked)
    return pl.pallas_call(
        functools.partial(_proj_kernel, n_alias=len(stacked), key_minor=seq is not None), grid=(nt,),
        in_specs=in_specs, out_specs=out_specs,
        out_shape=out_shape, input_output_aliases={n_in + a: 1 + a for a in range(len(stacked))},
        compiler_params=_params(("parallel",)), name="proj",
    )(h, g, w["w_in"], w["g_q"], w["w_nope"], w["w_rope_a"], w["w_rope_b"], w["g_kv"], w["w_uk_bd"], cos_t, sin_t,
      *stacked)


def _upper_ones(n):
    r = lax.broadcasted_iota(jnp.int32, (2 * n, 2 * n), 0) % n
    c = lax.broadcasted_iota(jnp.int32, (2 * n, 2 * n), 1)
    return jnp.where(jnp.logical_or(r > c, c >= n), 1.0, 0.0).astype(BF16)


def _sb_log_terms(z, upper_ones, visible):
    n = upper_ones.shape[0] // 2
    sp = jnp.log(1.0 + jnp.exp2(jnp.abs(z) * (-LOG2E)))
    log_beta = jnp.minimum(z, 0.0) - sp
    log_keep = log_beta - z
    if visible is not None:
        log_keep = jnp.where(visible, log_keep, 0.0)
    hi = log_keep.astype(BF16)
    lo = (log_keep - hi.astype(F32)).astype(BF16)
    later, total = [], None
    for t in reversed(range(z.shape[1] // n)):
        cols = slice(t * n, (t + 1) * n)
        sums = _dot(jnp.concatenate([hi[:, cols], lo[:, cols]], axis=1), upper_ones)
        later.insert(0, sums[:, :n] if total is None else sums[:, :n] + total)
        total = sums[:, n:] if total is None else total + sums[:, n:]
    return log_beta, later[0] if len(later) == 1 else jnp.concatenate(later, axis=1), total


def _sb_weights(log_beta, later, visible):
    a = jnp.exp(log_beta + later)
    if visible is not None:
        a = jnp.where(visible, a, 0.0)
    return a.astype(BF16)


def _sb_attn_kernel(q_ref, k_ref, v_ref, o_ref, qm_ref, acc_ref, carry_ref):
    qi = pl.program_id(1)
    pairs = SB_HEADS // 2
    rows = SB_HEADS * Q_BLOCK
    upper_ones = _upper_ones(LANES)
    left = lax.broadcasted_iota(jnp.int32, (Q_BLOCK, LANES), 1) < HEAD_DIM
    for p in range(pairs):
        q2 = q_ref[:, p * LANES:(p + 1) * LANES]
        zero = jnp.zeros_like(q2)
        qm_ref[p] = jnp.concatenate([jnp.where(left, q2, zero), jnp.where(left, zero, q2)], axis=0)

    groups = ((0, 1), (2, 3))
    grows = rows // len(groups)

    def block(kb, nblk, first):
        keys = pl.ds(pl.multiple_of(kb * Q_BLOCK, Q_BLOCK), nblk * Q_BLOCK)
        zs = [jnp.concatenate([_dot_nt(qm_ref[p], k_ref[keys, p * LANES:(p + 1) * LANES]) for p in g], axis=0)
              for g in groups]
        visible = None
        if first:
            r = lax.broadcasted_iota(jnp.int32, (grows, LANES), 0) % Q_BLOCK
            visible = lax.broadcasted_iota(jnp.int32, (grows, LANES), 1) < r
        terms = [_sb_log_terms(z, upper_ones, visible) for z in zs]
        for gi, g in enumerate(groups):
            log_beta, later, total = terms[gi]
            grp = slice(gi * grows, (gi + 1) * grows)
            if first:
                carry_ref[grp] = total
            else:
                carry = carry_ref[grp]
                later = later + jnp.concatenate([carry] * nblk, axis=1)
                carry_ref[grp] = carry + total
            a = _sb_weights(log_beta, later, visible)
            for i, p in enumerate(g):
                pv = _dot(a[i * 2 * Q_BLOCK:(i + 1) * 2 * Q_BLOCK], v_ref[keys, p * LANES:(p + 1) * LANES])
                acc_ref[p] = pv if first else acc_ref[p] + pv

    block(qi, 1, True)

    @pl.when(qi % 2 == 1)
    def _():
        block(qi - 1, 1, False)

    top = qi - qi % 2

    @pl.loop(0, qi // 2)
    def _(j):
        block(top - 2 - 2 * j, 2, False)

    for p in range(pairs):
        o_ref[:, p * LANES:(p + 1) * LANES] = jnp.where(left, acc_ref[p, :Q_BLOCK], acc_ref[p, Q_BLOCK:])


def _sb_attn(q, k16, v16, batch, seq):
    nq = seq // Q_BLOCK
    return pl.pallas_call(
        _sb_attn_kernel, grid=(batch, nq),
        in_specs=[pl.BlockSpec((Q_BLOCK, SB_WIDTH), lambda b, i: (b * nq + i, 0)),
                  pl.BlockSpec((seq, SB_WIDTH), lambda b, i: (b, 0)),
                  pl.BlockSpec((seq, SB_WIDTH), lambda b, i: (b, 0))],
        out_specs=pl.BlockSpec((Q_BLOCK, SB_WIDTH), lambda b, i: (b * nq + i, 0)),
        out_shape=jax.ShapeDtypeStruct((batch * seq, SB_WIDTH), F32),
        scratch_shapes=[pltpu.VMEM((SB_HEADS // 2, 2 * Q_BLOCK, LANES), BF16),
                        pltpu.VMEM((SB_HEADS // 2, 2 * Q_BLOCK, LANES), F32),
                        pltpu.VMEM((SB_HEADS * Q_BLOCK, LANES), F32)],
        compiler_params=_params(("parallel", "parallel")), name="sb_attn",
    )(q, k16, v16)


def _mla_attn_kernel(q_ref, kv_ref, latt_ref, o_ref, m_ref, l_ref, acc_ref):
    qi = pl.program_id(1)
    cols = MLA_HEADS * Q_BLOCK
    q = q_ref[0].reshape(cols, QCAT)
    key = lax.broadcasted_iota(jnp.int32, (Q_BLOCK, cols), 0)
    qry = lax.broadcasted_iota(jnp.int32, (Q_BLOCK, cols), 1) % Q_BLOCK

    def scores(start, size):
        kv = kv_ref[pl.ds(pl.multiple_of(start, Q_BLOCK), size), :]
        return _dot_nt(kv, q) * (MLA_SCALE * LOG2E)

    s = jnp.where(key <= qry, scores(qi * Q_BLOCK, Q_BLOCK), NEG_INF)
    m = jnp.max(s, axis=0, keepdims=True)
    p = jnp.exp2(s - m)
    m_ref[...] = m
    l_ref[...] = jnp.sum(p, axis=0, keepdims=True)
    acc_ref[...] = _dot(latt_ref[qi], p.astype(BF16))

    halves = (slice(0, cols // 2), slice(cols // 2, cols))

    def update(start, size, latt):
        kv = kv_ref[pl.ds(pl.multiple_of(start, Q_BLOCK), size), :]
        ss = [_dot_nt(kv, q[h]) * (MLA_SCALE * LOG2E) for h in halves]
        for h, s in zip(halves, ss):
            m_prev = m_ref[:, h]
            m_new = jnp.maximum(m_prev, jnp.max(s, axis=0, keepdims=True))
            alpha = jnp.exp2(m_prev - m_new)
            p = jnp.exp2(s - m_new)
            l_ref[:, h] = alpha * l_ref[:, h] + jnp.sum(p, axis=0, keepdims=True)
            acc_ref[:, h] = alpha * acc_ref[:, h] + _dot(latt, p.astype(BF16))
            m_ref[:, h] = m_new

    @pl.when(qi % 2 == 1)
    def _():
        update((qi - 1) * Q_BLOCK, Q_BLOCK, latt_ref[qi - 1])

    @pl.loop(0, qi // 2)
    def _(j):
        latt = jnp.concatenate([latt_ref[2 * j], latt_ref[2 * j + 1]], axis=1)
        update(j * 2 * Q_BLOCK, 2 * Q_BLOCK, latt)

    out = acc_ref[...] / l_ref[...]
    for hd in range(MLA_HEADS):
        blk = slice(hd * Q_BLOCK, (hd + 1) * Q_BLOCK)
        o_ref[:, hd * KV_RANK:(hd + 1) * KV_RANK] = out[:, blk].T.astype(BF16)


def _mla_attn(qcat, kvcat, latt, batch, seq):
    nq = seq // Q_BLOCK
    cols = MLA_HEADS * Q_BLOCK
    return pl.pallas_call(
        _mla_attn_kernel, grid=(batch, nq),
        in_specs=[pl.BlockSpec((1, MLA_HEADS, Q_BLOCK, QCAT), lambda b, i: (b * nq + i, 0, 0, 0)),
                  pl.BlockSpec((seq, QCAT), lambda b, i: (b, 0)),
                  pl.BlockSpec((nq, KV_RANK, Q_BLOCK), lambda b, i: (b, 0, 0))],
        out_specs=pl.BlockSpec((Q_BLOCK, MLA_HEADS * KV_RANK), lambda b, i: (b * nq + i, 0)),
        out_shape=jax.ShapeDtypeStruct((batch * seq, MLA_HEADS * KV_RANK), BF16),
        scratch_shapes=[pltpu.VMEM((1, cols), F32), pltpu.VMEM((1, cols), F32), pltpu.VMEM((KV_RANK, cols), F32)],
        compiler_params=_params(("parallel", "parallel")), name="mla_attn",
    )(qcat, kvcat, latt)


def _decode_kernel(pt_ref, qsb_ref, knew_ref, vnew_ref, qcat_ref, latnew_ref, krnew_ref, *refs):
    n = PAGES_PER_STEP
    kt_refs, vt_refs, lat_refs, krt_refs = refs[:n], refs[n:2 * n], refs[2 * n:3 * n], refs[3 * n:4 * n]
    osb_ref, omla_ref = refs[4 * n:4 * n + 2]
    acc_sb, carry_ref, m_ref, l_ref, acc_mla = refs[4 * n + 2:]
    step = pl.program_id(1)
    nq = qsb_ref.shape[0]
    rows = SB_HEADS * nq

    q_tiled = jnp.concatenate([qsb_ref[...]] * SB_HEADS, axis=0)
    row_head = lax.broadcasted_iota(jnp.int32, (rows, SB_WIDTH), 0) // nq
    lane_head = lax.broadcasted_iota(jnp.int32, (rows, SB_WIDTH), 1) // HEAD_DIM
    own_head = row_head == lane_head
    q_bd = jnp.where(own_head, q_tiled, 0.0).astype(BF16)
    qcat = qcat_ref[...].astype(BF16)
    q_lat = qcat[:, :KV_RANK]
    q_rope = qcat[:, KV_RANK:KV_RANK + MLA_ROPE]
    upper_ones = _upper_ones(PAGE)
    scale = MLA_SCALE * LOG2E

    @pl.when(step == 0)
    def _():
        pad = PAGE - nq
        q_idx = lax.broadcasted_iota(jnp.int32, (rows, PAGE), 0) % nq
        key_idx = lax.broadcasted_iota(jnp.int32, (rows, PAGE), 1)
        knew = jnp.concatenate([knew_ref[...], jnp.zeros((pad, SB_WIDTH), F32)], axis=0).astype(BF16)
        vnew = jnp.concatenate([vnew_ref[...], jnp.zeros((pad, SB_WIDTH), F32)], axis=0).astype(BF16)
        visible = key_idx < q_idx
        log_beta, later, total = _sb_log_terms(_dot_nt(q_bd, knew), upper_ones, visible)
        carry_ref[...] = total
        acc_sb[...] = _dot(_sb_weights(log_beta, later, visible), vnew)
        lat = jnp.concatenate([latnew_ref[...], jnp.zeros((pad, KV_RANK), F32)], axis=0).astype(BF16)
        kr = jnp.concatenate([krnew_ref[...], jnp.zeros((pad, MLA_ROPE), F32)], axis=0).astype(BF16)
        s = jnp.where(key_idx <= q_idx, (_dot_nt(q_lat, lat) + _dot_nt(q_rope, kr)) * scale, NEG_INF)
        m = jnp.max(s, axis=1, keepdims=True)
        p = jnp.exp2(s - m)
        m_ref[...] = m
        l_ref[...] = jnp.sum(p, axis=1, keepdims=True)
        acc_mla[...] = _dot(p.astype(BF16), lat)

    halves = [range(g, g + PAGE_GROUP) for g in range(0, n, PAGE_GROUP)]
    zs = [jnp.concatenate([_dot(q_bd, kt_refs[j][...].astype(BF16)) for j in half], axis=0) for half in halves]
    lats = [lat_refs[j][...].astype(BF16) for j in range(n)]
    s = jnp.concatenate([_dot_nt(q_lat, lats[j]) + _dot(q_rope, krt_refs[j][...].astype(BF16)) for j in range(n)],
                        axis=1) * scale
    terms = [_sb_log_terms(z, upper_ones, None) for z in zs]
    carry = carry_ref[...]
    acc = acc_sb[...]
    for half, (log_beta, later, total) in zip(halves, terms):
        for i, j in enumerate(half):
            page = slice(i * rows, (i + 1) * rows)
            a = _sb_weights(log_beta[page], later[page] + carry, None)
            carry = carry + total[page]
            acc = acc + _dot_nt(a, vt_refs[j][...].astype(BF16))
    carry_ref[...] = carry
    acc_sb[...] = acc

    m = m_ref[...]
    m_new = jnp.maximum(m, jnp.max(s, axis=1, keepdims=True))
    alpha = jnp.exp2(m - m_new)
    p = jnp.exp2(s - m_new)
    l = alpha * l_ref[...] + jnp.sum(p, axis=1, keepdims=True)
    p = p.astype(BF16)
    accm = alpha * acc_mla[...]
    for j in range(n):
        accm = accm + _dot(p[:, j * PAGE:(j + 1) * PAGE], lats[j])
    m_ref[...] = m_new
    l_ref[...] = l
    acc_mla[...] = accm

    @pl.when(step == pl.num_programs(1) - 1)
    def _():
        own = jnp.where(own_head, acc, 0.0)
        out = own[:nq]
        for hd in range(1, SB_HEADS):
            out = out + own[hd * nq:(hd + 1) * nq]
        osb_ref[...] = out
        omla_ref[...] = accm / l


def _decode(layer, page_table, qsb, knew, vnew, qcat, latnew, krnew, ckt, cvt, clat, ckrt):
    nb, nq, _ = qsb.shape
    n_pages = page_table.shape[1]
    n = PAGES_PER_STEP
    steps = n_pages // n
    rows = SB_HEADS * nq

    def per_seq(shape):
        return pl.BlockSpec((None,) + shape, lambda b, s, pt: (b, 0, 0))

    def page_spec(shape, j):
        return pl.BlockSpec((None, None) + shape,
                            lambda b, s, pt, j=j: (layer, pt[b, n_pages - 1 - (s * n + j)], 0, 0))

    in_specs = [per_seq((nq, SB_WIDTH)), per_seq((nq, SB_WIDTH)), per_seq((nq, SB_WIDTH)), per_seq((rows, QCAT)),
                per_seq((nq, KV_RANK)), per_seq((nq, MLA_ROPE))]
    for shape in ((SB_WIDTH, PAGE), (SB_WIDTH, PAGE), (PAGE, KV_RANK), (MLA_ROPE, PAGE)):
        in_specs += [page_spec(shape, j) for j in range(n)]
    grid_spec = pltpu.PrefetchScalarGridSpec(
        num_scalar_prefetch=1, grid=(nb, steps), in_specs=in_specs,
        out_specs=(per_seq((nq, SB_WIDTH)), per_seq((rows, KV_RANK))),
        scratch_shapes=[pltpu.VMEM((rows, SB_WIDTH), F32), pltpu.VMEM((rows, PAGE), F32), pltpu.VMEM((rows, 1), F32),
                        pltpu.VMEM((rows, 1), F32), pltpu.VMEM((rows, KV_RANK), F32)])
    return pl.pallas_call(
        _decode_kernel, grid_spec=grid_spec,
        out_shape=(jax.ShapeDtypeStruct((nb, nq, SB_WIDTH), F32), jax.ShapeDtypeStruct((nb, rows, KV_RANK), F32)),
        compiler_params=_params(("parallel", "arbitrary")), name="decode_attn",
    )(page_table, qsb, knew, vnew, qcat, latnew, krnew, *([ckt] * n), *([cvt] * n), *([clat] * n), *([ckrt] * n))


def _post_kernel(osb_ref, omla_ref, h_ref, gsb_ref, gmla_ref, wuv_ref, wout_ref, gffn_ref, h1_ref, hn_ref):
    pairs = [_dot(omla_ref[:, p * 2 * KV_RANK:(p + 1) * 2 * KV_RANK], wuv_ref[p]) for p in range(MLA_HEADS // 2)]
    o_mla = jnp.concatenate(pairs, axis=1)
    mixed = jnp.concatenate([_rms(osb_ref[...], gsb_ref[...]), _rms(o_mla, gmla_ref[...])], axis=1).astype(BF16)
    h1 = h_ref[...] + _dot(mixed, wout_ref[...])
    h1_ref[...] = h1
    hn_ref[...] = _rms(h1, gffn_ref[...]).astype(BF16)


def _post(osb, omla, h, w, tm):
    t = h.shape[0]
    row = lambda n: pl.BlockSpec((tm, n), lambda i: (i, 0))
    return pl.pallas_call(
        _post_kernel, grid=(t // tm,),
        in_specs=[row(SB_WIDTH), row(MLA_HEADS * KV_RANK), row(D_MODEL), _full((1, SB_WIDTH)), _full((1, MLA_WIDTH)),
                  _full((MLA_HEADS // 2, 2 * KV_RANK, 2 * MLA_V)), _full((D_MODEL, D_MODEL)), _full((1, D_MODEL))],
        out_specs=(row(D_MODEL), row(D_MODEL)),
        out_shape=(jax.ShapeDtypeStruct((t, D_MODEL), F32), jax.ShapeDtypeStruct((t, D_MODEL), BF16)),
        compiler_params=_params(("parallel",)), name="post_attn",
    )(osb, omla, h, w["g_out_sb"], w["g_out_mla"], w["w_uv_bd"], w["w_out"], w["g_ffn"])


def _swiglu_chunk(x, wg, wu, wd):
    g = _dot(x, wg)
    act = (g * jax.nn.sigmoid(g) * _dot(x, wu)).astype(BF16)
    return _dot(act, wd)


def _ffn_kernel(hn_ref, h1_ref, wg_ref, wu_ref, wd_ref, gfin_ref, o_ref, acc_ref, *, final_norm):
    j = pl.program_id(1)

    @pl.when(j == 0)
    def _():
        acc_ref[...] = jnp.zeros_like(acc_ref)

    acc_ref[...] += _swiglu_chunk(hn_ref[...], wg_ref[...], wu_ref[...], wd_ref[...])

    @pl.when(j == pl.num_programs(1) - 1)
    def _():
        out = h1_ref[...] + acc_ref[...]
        o_ref[...] = _rms(out, gfin_ref[...]) if final_norm else out


def _ffn(hn, h1, wg, wu, wd, g_final, tm, tf, final_norm):
    t = hn.shape[0]
    f = wg.shape[1]
    return pl.pallas_call(
        functools.partial(_ffn_kernel, final_norm=final_norm), grid=(t // tm, f // tf),
        in_specs=[pl.BlockSpec((tm, D_MODEL), lambda i, j: (i, 0)), pl.BlockSpec((tm, D_MODEL), lambda i, j: (i, 0)),
                  pl.BlockSpec((D_MODEL, tf), lambda i, j: (0, j)), pl.BlockSpec((D_MODEL, tf), lambda i, j: (0, j)),
                  pl.BlockSpec((tf, D_MODEL), lambda i, j: (j, 0)), _full((1, D_MODEL))],
        out_specs=pl.BlockSpec((tm, D_MODEL), lambda i, j: (i, 0)),
        out_shape=jax.ShapeDtypeStruct((t, D_MODEL), F32),
        scratch_shapes=[pltpu.VMEM((tm, D_MODEL), F32)],
        compiler_params=_params(("parallel", "arbitrary")), name="ffn_dense",
    )(hn, h1, wg, wu, wd, g_final)


def _router_kernel(h1_ref, gffn_ref, wr_ref, comb_ref, meta_ref):
    hn = _rms(h1_ref[...], gffn_ref[...])
    w = wr_ref[...]
    hn_hi, w_hi = hn.astype(BF16), w.astype(BF16)
    hn_lo, w_lo = (hn - hn_hi.astype(F32)).astype(BF16), (w - w_hi.astype(F32)).astype(BF16)
    logits = _dot(hn_hi, w_hi) + _dot(hn_lo, w_hi) + _dot(hn_hi, w_lo)
    lane = lax.broadcasted_iota(jnp.int32, logits.shape, 1)
    logits = jnp.where(lane < N_EXPERTS, logits, NEG_INF)
    m1 = jnp.max(logits, axis=1, keepdims=True)
    i1 = jnp.min(jnp.where(logits == m1, lane, LANES), axis=1, keepdims=True)
    rest = jnp.where(lane == i1, NEG_INF, logits)
    m2 = jnp.max(rest, axis=1, keepdims=True)
    i2 = jnp.min(jnp.where(rest == m2, lane, LANES), axis=1, keepdims=True)
    e2 = jnp.exp(m2 - m1)
    g1 = 1.0 / (1.0 + e2)
    g2 = e2 / (1.0 + e2)
    comb_ref[...] = jnp.where(lane == i1, g1, 0.0) + jnp.where(lane == i2, g2, 0.0)
    meta_ref[...] = jnp.where(lane == 0, i1.astype(F32), jnp.where(lane == 1, i2.astype(F32),
                              jnp.where(lane == 2, g1, jnp.where(lane == 3, g2, 0.0))))


def _router(h1, g_ffn, w_router, tm):
    t = h1.shape[0]
    return pl.pallas_call(
        _router_kernel, grid=(t // tm,),
        in_specs=[pl.BlockSpec((tm, D_MODEL), lambda i: (i, 0)), _full((1, D_MODEL)), _full((D_MODEL, LANES))],
        out_specs=(pl.BlockSpec((tm, LANES), lambda i: (i, 0)), pl.BlockSpec((tm, LANES), lambda i: (i, 0))),
        out_shape=(jax.ShapeDtypeStruct((t, LANES), F32), jax.ShapeDtypeStruct((t, LANES), F32)),
        compiler_params=_params(("parallel",)), name="router",
    )(h1, g_ffn, w_router)


def _moe_kernel(hn_ref, h1_ref, comb_ref, wg_ref, wu_ref, wd_ref, gfin_ref, o_ref, acc_ref, *, final_norm):
    e = pl.program_id(1)
    j = pl.program_id(2)

    @pl.when(jnp.logical_and(e == 0, j == 0))
    def _():
        acc_ref[...] = jnp.zeros_like(acc_ref)

    lane = lax.broadcasted_iota(jnp.int32, comb_ref.shape, 1)
    gate = jnp.sum(jnp.where(lane == e, comb_ref[...], 0.0), axis=1, keepdims=True)
    acc_ref[...] += gate * _swiglu_chunk(hn_ref[...], wg_ref[...], wu_ref[...], wd_ref[...])

    @pl.when(jnp.logical_and(e == pl.num_programs(1) - 1, j == pl.num_programs(2) - 1))
    def _():
        out = h1_ref[...] + acc_ref[...]
        o_ref[...] = _rms(out, gfin_ref[...]) if final_norm else out


def _moe(hn, h1, comb, wg, wu, wd, g_final, tm, tf, final_norm):
    t = hn.shape[0]
    ne, _, f = wg.shape
    tok = lambda n: pl.BlockSpec((tm, n), lambda i, e, j: (i, 0))
    return pl.pallas_call(
        functools.partial(_moe_kernel, final_norm=final_norm), grid=(t // tm, ne, f // tf),
        in_specs=[tok(D_MODEL), tok(D_MODEL), tok(LANES),
                  pl.BlockSpec((None, D_MODEL, tf), lambda i, e, j: (e, 0, j)),
                  pl.BlockSpec((None, D_MODEL, tf), lambda i, e, j: (e, 0, j)),
                  pl.BlockSpec((None, tf, D_MODEL), lambda i, e, j: (e, j, 0)), _full((1, D_MODEL))],
        out_specs=tok(D_MODEL),
        out_shape=jax.ShapeDtypeStruct((t, D_MODEL), F32),
        scratch_shapes=[pltpu.VMEM((tm, D_MODEL), F32)],
        compiler_params=_params(("parallel", "arbitrary", "arbitrary")), name="ffn_moe",
    )(hn, h1, comb, wg, wu, wd, g_final)


def _routing_tables(e1, e2, n_tiles):
    t = e1.shape[0]
    tr = MOE_ROWS
    flat = jnp.stack([e1, e2], axis=1).reshape(2 * t)
    order = jnp.argsort(flat, stable=True).astype(jnp.int32)
    counts = jnp.sum(flat[:, None] == jnp.arange(N_EXPERTS, dtype=jnp.int32)[None, :], axis=0).astype(jnp.int32)
    padded = (counts + tr - 1) // tr * tr
    pend = jnp.cumsum(padded)
    pstart = pend - padded
    ustart = jnp.cumsum(counts) - counts
    n_valid = (pend[-1] // tr).astype(jnp.int32)
    tile_start = jnp.arange(n_tiles, dtype=jnp.int32) * tr
    tile_expert = jnp.minimum(jnp.sum(tile_start[:, None] >= pend[None, :], axis=1), N_EXPERTS - 1).astype(jnp.int32)
    slot = jnp.arange(n_tiles * tr, dtype=jnp.int32)
    e = jnp.repeat(tile_expert, tr)
    rank = slot - pstart[e]
    valid = jnp.logical_and(slot < pend[-1], rank < counts[e])
    a = order[jnp.clip(ustart[e] + rank, 0, 2 * t - 1)]
    src = jnp.where(valid, a // 2, 0)
    dst = jnp.where(valid, (a % 2) * t + a // 2, 2 * t + e * tr + jnp.clip(rank - counts[e], 0, tr - 1))
    dst = jnp.concatenate([2 * t + N_EXPERTS * tr + jnp.arange(tr, dtype=jnp.int32), dst])
    return src.reshape(n_tiles, 1, tr), dst.reshape(n_tiles + 1, 1, tr), tile_expert, n_valid.reshape(1)


def _moe_grouped_kernel(te_ref, nv_ref, src_cur, src_next, dst_prev, dst_cur, h1_hbm, gffn_ref, wg_ref, wu_ref, wd_ref,
                        y_hbm, xbuf, xn_ref, acc_ref, obuf, gsem, ssem):
    i = pl.program_id(0)
    j = pl.program_id(1)
    nj = pl.num_programs(1)
    nv = nv_ref[0]
    tr = xn_ref.shape[0]
    ch = tr // MOE_CHUNKS
    slot = i % 2
    other = 1 - slot

    def gather_row(tok_ref, r, s):
        return pltpu.make_async_copy(h1_hbm.at[pl.ds(tok_ref[0, r], 1)], xbuf.at[s, pl.ds(r, 1)], gsem.at[s])

    def scatter_row(row_ref, r, s):
        return pltpu.make_async_copy(obuf.at[s, pl.ds(r, 1)], y_hbm.at[pl.ds(row_ref[0, r], 1)], ssem.at[s])

    @pl.when(i < nv)
    def _():
        @pl.when(jnp.logical_and(i == 0, j == 0))
        def _():
            obuf[...] = jnp.zeros_like(obuf)
            for r in range(tr):
                gather_row(src_cur, r, 0).start()
            for r in range(tr):
                gather_row(src_cur, r, 0).wait()

        @pl.when(j == 0)
        def _():
            xn_ref[...] = _rms(xbuf[slot], gffn_ref[...]).astype(BF16)
            acc_ref[...] = jnp.zeros_like(acc_ref)

        for c in range(ch):
            gather_row(src_next, j * ch + c, other).start()
        for c in range(ch):
            scatter_row(dst_prev, j * ch + c, other).start()

        acc_ref[...] += _swiglu_chunk(xn_ref[...], wg_ref[...], wu_ref[...], wd_ref[...])

        @pl.when(j == nj - 1)
        def _():
            obuf[slot] = acc_ref[...]
            for r in range(tr):
                gather_row(src_next, r, other).wait()
            for r in range(tr):
                scatter_row(dst_prev, r, other).wait()

            @pl.when(i == nv - 1)
            def _():
                for r in range(tr):
                    scatter_row(dst_cur, r, slot).start()
                for r in range(tr):
                    scatter_row(dst_cur, r, slot).wait()


def _moe_grouped(h1, g_ffn, src, dst, tile_expert, n_valid, wg, wu, wd):
    t = h1.shape[0]
    n_tiles, _, tr = src.shape
    ne, _, f = wg.shape
    tf = f // MOE_CHUNKS
    last = n_tiles - 1
    smem = functools.partial(pl.BlockSpec, memory_space=pltpu.SMEM)
    in_specs = [
        smem((None, 1, tr), lambda i, j, te, nv: (i, 0, 0)),
        smem((None, 1, tr), lambda i, j, te, nv: (jnp.minimum(i + 1, last), 0, 0)),
        smem((None, 1, tr), lambda i, j, te, nv: (i, 0, 0)),
        smem((None, 1, tr), lambda i, j, te, nv: (i + 1, 0, 0)),
        pl.BlockSpec(memory_space=pl.ANY),
        pl.BlockSpec((1, D_MODEL), lambda i, j, te, nv: (0, 0)),
        pl.BlockSpec((None, D_MODEL, tf), lambda i, j, te, nv: (te[i], 0, j)),
        pl.BlockSpec((None, D_MODEL, tf), lambda i, j, te, nv: (te[i], 0, j)),
        pl.BlockSpec((None, tf, D_MODEL), lambda i, j, te, nv: (te[i], j, 0)),
    ]
    grid_spec = pltpu.PrefetchScalarGridSpec(
        num_scalar_prefetch=2, grid=(n_tiles, MOE_CHUNKS), in_specs=in_specs,
        out_specs=pl.BlockSpec(memory_space=pl.ANY),
        scratch_shapes=[pltpu.VMEM((2, tr, D_MODEL), F32), pltpu.VMEM((tr, D_MODEL), BF16),
                        pltpu.VMEM((tr, D_MODEL), F32), pltpu.VMEM((2, tr, D_MODEL), F32),
                        pltpu.SemaphoreType.DMA((2,)), pltpu.SemaphoreType.DMA((2,))])
    return pl.pallas_call(
        _moe_grouped_kernel, grid_spec=grid_spec,
        out_shape=jax.ShapeDtypeStruct((2 * t + (ne + 1) * tr, D_MODEL), F32),
        compiler_params=_params(("arbitrary", "arbitrary")), name="ffn_moe_grouped",
    )(tile_expert, n_valid, src, src, dst, dst, h1, g_ffn, wg, wu, wd)


def _moe_combine_kernel(h1_ref, y1_ref, y2_ref, meta_ref, gfin_ref, o_ref, *, final_norm):
    meta = meta_ref[...]
    lane = lax.broadcasted_iota(jnp.int32, meta.shape, 1)
    g1 = jnp.sum(jnp.where(lane == 2, meta, 0.0), axis=1, keepdims=True)
    g2 = jnp.sum(jnp.where(lane == 3, meta, 0.0), axis=1, keepdims=True)
    out = h1_ref[...] + g1 * y1_ref[...] + g2 * y2_ref[...]
    o_ref[...] = _rms(out, gfin_ref[...]) if final_norm else out


def _moe_combine(h1, y, meta, g_final, tm, final_norm):
    t = h1.shape[0]
    nt = t // tm
    return pl.pallas_call(
        functools.partial(_moe_combine_kernel, final_norm=final_norm), grid=(nt,),
        in_specs=[pl.BlockSpec((tm, D_MODEL), lambda i: (i, 0)), pl.BlockSpec((tm, D_MODEL), lambda i: (i, 0)),
                  pl.BlockSpec((tm, D_MODEL), lambda i: (i + nt, 0)), pl.BlockSpec((tm, LANES), lambda i: (i, 0)),
                  _full((1, D_MODEL))],
        out_specs=pl.BlockSpec((tm, D_MODEL), lambda i: (i, 0)),
        out_shape=jax.ShapeDtypeStruct((t, D_MODEL), F32),
        compiler_params=_params(("parallel",)), name="moe_combine",
    )(h1, y, y, meta, g_final)


def _moe_routed(h1, g_ffn, w_router, wg, wu, wd, g_final, final_norm):
    t = h1.shape[0]
    _, meta = _router(h1, g_ffn, w_router, 1024)
    e1 = meta[:, 0].astype(jnp.int32)
    e2 = meta[:, 1].astype(jnp.int32)
    n_tiles = -(-2 * t // MOE_ROWS) + N_EXPERTS
    src, dst, tile_expert, n_valid = _routing_tables(e1, e2, n_tiles)
    y = _moe_grouped(h1, g_ffn, src, dst, tile_expert, n_valid, wg, wu, wd)
    return _moe_combine(h1, y, meta, g_final, 512, final_norm)


def _prep_layer(l, g_attn_norm, w_in, g_q_norm, w_uq, g_kv_norm, w_uk, w_uv, g_out_sb, g_out_mla, w_out, g_ffn_norm):
    half = MLA_ROPE // 2
    o = 3 * SB_WIDTH + Q_RANK + KV_RANK
    kr = w_in[l][:, o:]
    pad = jnp.zeros((D_MODEL, LANES - MLA_ROPE), F32)
    kr_a = jnp.concatenate([kr, pad], axis=1)
    kr_b = jnp.concatenate([-kr[:, half:], kr[:, :half], pad], axis=1)
    w_in_p = jnp.concatenate([w_in[l][:, :o], kr_a, kr_b], axis=1).astype(BF16)

    uq = w_uq[l]
    w_nope = uq[:, :, :MLA_NOPE].reshape(Q_RANK, MLA_HEADS * MLA_NOPE).astype(BF16)
    qr = uq[:, :, MLA_NOPE:]
    hpad = jnp.zeros((Q_RANK, MLA_HEADS, LANES - MLA_ROPE), F32)
    w_rope_a = jnp.concatenate([qr, hpad], axis=2).reshape(Q_RANK, MLA_HEADS * LANES).astype(BF16)
    w_rope_b = jnp.concatenate([-qr[:, :, half:], qr[:, :, :half], hpad], axis=2)
    w_rope_b = w_rope_b.reshape(Q_RANK, MLA_HEADS * LANES).astype(BF16)

    uk = jnp.transpose(w_uk[l], (1, 2, 0)).reshape(MLA_HEADS // 2, 2, MLA_NOPE, KV_RANK)
    zk = jnp.zeros((MLA_HEADS // 2, MLA_NOPE, KV_RANK), F32)
    w_uk_bd = jnp.concatenate([jnp.concatenate([uk[:, 0], zk], axis=2),
                               jnp.concatenate([zk, uk[:, 1]], axis=2)], axis=1).astype(BF16)
    uv = jnp.transpose(w_uv[l], (1, 0, 2)).reshape(MLA_HEADS // 2, 2, KV_RANK, MLA_V)
    zv = jnp.zeros((MLA_HEADS // 2, KV_RANK, MLA_V), F32)
    w_uv_bd = jnp.concatenate([jnp.concatenate([uv[:, 0], zv], axis=2),
                               jnp.concatenate([zv, uv[:, 1]], axis=2)], axis=1).astype(BF16)
    return {
        "g_attn": g_attn_norm[l][None], "w_in": w_in_p, "g_q": g_q_norm[l][None], "w_nope": w_nope,
        "w_rope_a": w_rope_a, "w_rope_b": w_rope_b, "g_kv": g_kv_norm[l][None], "w_uk_bd": w_uk_bd,
        "w_uv_bd": w_uv_bd, "g_out_sb": g_out_sb[l][None], "g_out_mla": g_out_mla[l][None],
        "w_out": w_out[l].astype(BF16), "g_ffn": g_ffn_norm[l][None],
    }


def _rope_tables(pos):
    inv_freq = 1.0 / (ROPE_THETA ** (jnp.arange(0, MLA_ROPE, 2, dtype=F32) / MLA_ROPE))
    ang = pos.astype(F32)[:, None] * inv_freq[None, :]
    pad = jnp.zeros((pos.shape[0], LANES - MLA_ROPE), F32)
    cos, sin = jnp.cos(ang), jnp.sin(ang)
    return jnp.concatenate([cos, cos, pad], axis=1), jnp.concatenate([sin, sin, pad], axis=1)


def kernel(x_prompt, x_sample, cache_sb_k, cache_sb_v, cache_mla_latent, cache_mla_krope, page_table, g_attn_norm,
           w_in, g_q_norm, w_uq, g_kv_norm, w_uk, w_uv, g_out_sb, g_out_mla, w_out, g_ffn_norm, w_gate_dense,
           w_up_dense, w_down_dense, w_router, w_gate_moe, w_up_moe, w_down_moe, g_final):
    batch, seq, _ = x_prompt.shape
    nb, nq, _ = x_sample.shape
    depth, n_pool = cache_sb_k.shape[:2]
    past_len = page_table.shape[1] * PAGE
    tp, ts = batch * seq, nb * nq

    ckt = jnp.transpose(cache_sb_k, (0, 1, 3, 4, 2)).reshape(depth, n_pool, SB_WIDTH, PAGE)
    cvt = jnp.transpose(cache_sb_v, (0, 1, 3, 4, 2)).reshape(depth, n_pool, SB_WIDTH, PAGE)
    ckrt = jnp.transpose(cache_mla_krope, (0, 1, 3, 2))
    cos_p, sin_p = _rope_tables(jnp.tile(jnp.arange(seq, dtype=jnp.int32), batch))
    cos_s, sin_s = _rope_tables(jnp.tile(past_len + jnp.arange(nq, dtype=jnp.int32), nb))
    g_fin = g_final[None]

    h_p = x_prompt.reshape(tp, D_MODEL)
    h_s = x_sample.reshape(ts, D_MODEL)
    rows_p, rows_s = [], []
    kv_stacked = ()
    for l in range(depth):
        w = _prep_layer(l, g_attn_norm, w_in, g_q_norm, w_uq, g_kv_norm, w_uk, w_uv, g_out_sb, g_out_mla, w_out,
                        g_ffn_norm)
        last = l == depth - 1

        q, k_all, v_all, k16, v16, lat, kr, kvcat, latt, qcat = _proj(h_p, w["g_attn"], w, cos_p, sin_p, 512, l, depth,
                                                                      kv_stacked, seq)
        kv_stacked = (k_all, v_all)
        rows_p.append((lat, kr))
        o_sb = _sb_attn(q, k16, v16, batch, seq)
        o_mla = _mla_attn(qcat, kvcat, latt, batch, seq)
        h1_p, hn_p = _post(o_sb, o_mla, h_p, w, 512)

        q, k, v, _, _, lat, kr, _, _, qcat = _proj(h_s, w["g_attn"], w, cos_s, sin_s, ts)
        rows_s.append((k, v, lat, kr))
        qcat_s = qcat.reshape(ts // Q_BLOCK, MLA_HEADS, Q_BLOCK // nq, nq, QCAT)
        qcat_s = jnp.transpose(qcat_s, (0, 2, 1, 3, 4)).reshape(nb, MLA_HEADS * nq, QCAT).astype(F32)
        o_sb, o_mla = _decode(l, page_table, q.astype(F32).reshape(nb, nq, SB_WIDTH), k.reshape(nb, nq, SB_WIDTH),
                              v.reshape(nb, nq, SB_WIDTH), qcat_s, lat.reshape(nb, nq, KV_RANK),
                              kr.reshape(nb, nq, MLA_ROPE), ckt, cvt, cache_mla_latent, ckrt)
        o_mla = jnp.transpose(o_mla.reshape(nb, MLA_HEADS, nq, KV_RANK), (0, 2, 1, 3))
        o_mla = o_mla.reshape(ts, MLA_HEADS * KV_RANK).astype(BF16)
        h1_s, hn_s = _post(o_sb.reshape(ts, SB_WIDTH), o_mla, h_s, w, ts)

        i = l // 2
        if l % 2 == 0:
            wg, wu, wd = w_gate_dense[i].astype(BF16), w_up_dense[i].astype(BF16), w_down_dense[i].astype(BF16)
            h_p = _ffn(hn_p, h1_p, wg, wu, wd, g_fin, 1024, 512, last)
            h_s = _ffn(hn_s, h1_s, wg, wu, wd, g_fin, ts, 512, last)
        else:
            wg, wu, wd = w_gate_moe[i].astype(BF16), w_up_moe[i].astype(BF16), w_down_moe[i].astype(BF16)
            wr = jnp.concatenate([w_router[i], jnp.zeros((D_MODEL, LANES - N_EXPERTS), F32)], axis=1)
            h_p = _moe_routed(h1_p, w["g_ffn"], wr, wg, wu, wd, g_fin, last)
            h_s = _moe(hn_s, h1_s, _router(h1_s, w["g_ffn"], wr, ts)[0], wg, wu, wd, g_fin, ts, 1792, last)

    y_prompt = h_p.reshape(batch, seq, D_MODEL)
    y_sample = h_s.reshape(nb, nq, D_MODEL)

    def stack(rows, idx, shape):
        return jnp.stack([r[idx].reshape(shape) for r in rows])

    def key_minor_rows(x):
        return jnp.transpose(x.reshape(depth, batch, SB_HEADS, HEAD_DIM, seq), (0, 1, 4, 2, 3))

    return (y_prompt, y_sample,
            key_minor_rows(kv_stacked[0]), key_minor_rows(kv_stacked[1]),
            stack(rows_p, 0, (batch, seq, KV_RANK)), stack(rows_p, 1, (batch, seq, MLA_ROPE)),
            stack(rows_s, 0, (nb, nq, SB_HEADS, HEAD_DIM)), stack(rows_s, 1, (nb, nq, SB_HEADS, HEAD_DIM)),
            stack(rows_s, 2, (nb, nq, KV_RANK)), stack(rows_s, 3, (nb, nq, MLA_ROPE)))
```

```python
import functools

import jax
import jax.numpy as jnp
from jax import lax
from jax.experimental import pallas as pl
from jax.experimental.pallas import tpu as pltpu

F32 = jnp.float32
BF16 = jnp.bfloat16

D_MODEL = 1024
HEAD_DIM = 64
SB_HEADS = 8
SB_WIDTH = 512
MLA_HEADS = 8
MLA_NOPE = 64
MLA_ROPE = 32
MLA_V = 64
MLA_WIDTH = 512
Q_RANK = 256
KV_RANK = 128
ROPE_THETA = 10000.0
N_EXPERTS = 8
PAGE = 128
Q_BLOCK = 128
EPS = 1e-6
SB_SCALE = HEAD_DIM ** -0.5
MLA_SCALE = (MLA_NOPE + MLA_ROPE) ** -0.5
LOG2E = 1.4426950408889634

LANES = 128
QCAT = 2 * LANES
IN_COLS_PADDED = 3 * SB_WIDTH + Q_RANK + KV_RANK + 2 * LANES
VMEM_LIMIT = 52 * 1024 * 1024
PAGES_PER_STEP = 32
PAGE_GROUP = 8
MOE_CHUNKS = 7
MOE_ROWS = 64 * MOE_CHUNKS
NEG_INF = float("-inf")

_NT = (((1,), (1,)), ((), ()))


def _dot(a, b):
    return jnp.dot(a, b, preferred_element_type=F32)


def _dot_nt(a, b):
    return lax.dot_general(a, b, _NT, preferred_element_type=F32)


def _rms(x, g):
    return x * lax.rsqrt(jnp.mean(x * x, axis=-1, keepdims=True) + EPS) * g


def _params(sem):
    return pltpu.CompilerParams(dimension_semantics=sem, vmem_limit_bytes=VMEM_LIMIT)


def _full(shape):
    return pl.BlockSpec(shape, lambda *_: (0,) * len(shape))


def _proj_kernel(*refs, n_alias, key_minor):
    (h_ref, g_ref, win_ref, gq_ref, wnope_ref, wra_ref, wrb_ref, gkv_ref, wuk_ref, cos_ref, sin_ref) = refs[:11]
    (q_ref, k_ref, v_ref, k16_ref, v16_ref, lat_ref, kr_ref, kvcat_ref, latt_ref, qcat_ref) = refs[11 + n_alias:]
    tm = h_ref.shape[0]
    xn = _rms(h_ref[...], g_ref[...]).astype(BF16)
    qkv = _dot(xn, win_ref[:, :3 * SB_WIDTH])
    q_ref[...] = (qkv[:, :SB_WIDTH] * SB_SCALE).astype(BF16)
    k = qkv[:, SB_WIDTH:2 * SB_WIDTH]
    v = qkv[:, 2 * SB_WIDTH:]
    k_ref[...] = k.T if key_minor else k
    v_ref[...] = v.T if key_minor else v
    k16_ref[...] = k.astype(BF16)
    v16_ref[...] = v.astype(BF16)

    rest = _dot(xn, win_ref[:, 3 * SB_WIDTH:])
    cos = cos_ref[...]
    sin = sin_ref[...]
    cqn = _rms(rest[:, :Q_RANK], gq_ref[...]).astype(BF16)
    lat = _rms(rest[:, Q_RANK:Q_RANK + KV_RANK], gkv_ref[...])
    o = Q_RANK + KV_RANK
    krope = rest[:, o:o + LANES] * cos + rest[:, o + LANES:] * sin
    lat_ref[...] = lat
    kr_ref[...] = krope[:, :MLA_ROPE]
    kvcat_ref[:, :LANES] = lat.astype(BF16)
    kvcat_ref[:, LANES:] = krope.astype(BF16)
    for i in range(tm // Q_BLOCK):
        latt_ref[i] = lat[i * Q_BLOCK:(i + 1) * Q_BLOCK].T.astype(BF16)

    qnope = _dot(cqn, wnope_ref[...]).astype(BF16)
    ra = _dot(cqn, wra_ref[...])
    rb = _dot(cqn, wrb_ref[...])
    for p in range(MLA_HEADS // 2):
        ql = _dot(qnope[:, p * LANES:(p + 1) * LANES], wuk_ref[p]).astype(BF16)
        for s in range(2):
            hd = 2 * p + s
            qr = (ra[:, hd * LANES:(hd + 1) * LANES] * cos + rb[:, hd * LANES:(hd + 1) * LANES] * sin).astype(BF16)
            for i in range(tm // Q_BLOCK):
                rows = slice(i * Q_BLOCK, (i + 1) * Q_BLOCK)
                qcat_ref[i, hd, :, :LANES] = ql[rows, s * LANES:(s + 1) * LANES]
                qcat_ref[i, hd, :, LANES:] = qr[rows]


def _proj(h, g, w, cos_t, sin_t, tm, layer=0, depth=1, stacked=(), seq=None):
    t = h.shape[0]
    nt = t // tm
    row = lambda n: pl.BlockSpec((tm, n), lambda i: (i, 0))
    if seq is None:
        kv_shape = jax.ShapeDtypeStruct((depth * t, SB_WIDTH), F32)
        layer_row = pl.BlockSpec((tm, SB_WIDTH), lambda i: (layer * nt + i, 0))
    else:
        per_seq = seq // tm
        kv_shape = jax.ShapeDtypeStruct((depth * (t // seq), SB_WIDTH, seq), F32)
        layer_row = pl.BlockSpec((None, SB_WIDTH, tm), lambda i: (layer * (t // seq) + i // per_seq, 0, i % per_seq))
    out_shape = (
        jax.ShapeDtypeStruct((t, SB_WIDTH), BF16),
        kv_shape,
        kv_shape,
        jax.ShapeDtypeStruct((t, SB_WIDTH), BF16),
        jax.ShapeDtypeStruct((t, SB_WIDTH), BF16),
        jax.ShapeDtypeStruct((t, KV_RANK), F32),
        jax.ShapeDtypeStruct((t, MLA_ROPE), F32),
        jax.ShapeDtypeStruct((t, QCAT), BF16),
        jax.ShapeDtypeStruct((t // Q_BLOCK, KV_RANK, Q_BLOCK), BF16),
        jax.ShapeDtypeStruct((t // Q_BLOCK, MLA_HEADS, Q_BLOCK, QCAT), BF16),
    )
    out_specs = (row(SB_WIDTH), layer_row, layer_row, row(SB_WIDTH), row(SB_WIDTH), row(KV_RANK),
                 row(MLA_ROPE), row(QCAT),
                 pl.BlockSpec((tm // Q_BLOCK, KV_RANK, Q_BLOCK), lambda i: (i, 0, 0)),
                 pl.BlockSpec((tm // Q_BLOCK, MLA_HEADS, Q_BLOCK, QCAT), lambda i: (i, 0, 0, 0)))
    in_specs = [row(D_MODEL), _full((1, D_MODEL)), _full((D_MODEL, IN_COLS_PADDED)), _full((1, Q_RANK)),
                _full((Q_RANK, MLA_HEADS * MLA_NOPE)), _full((Q_RANK, MLA_HEADS * LANES)),
                _full((Q_RANK, MLA_HEADS * LANES)), _full((1, KV_RANK)),
                _full((MLA_HEADS // 2, LANES, 2 * LANES)), row(LANES), row(LANES)]
    n_in = len(in_specs)
    in_specs += [pl.BlockSpec(memory_space=pl.ANY)] * len(stacked)
    return pl.pallas_call(
        functools.partial(_proj_kernel, n_alias=len(stacked), key_minor=seq is not None), grid=(nt,),
        in_specs=in_specs, out_specs=out_specs,
        out_shape=out_shape, input_output_aliases={n_in + a: 1 + a for a in range(len(stacked))},
        compiler_params=_params(("parallel",)), name="proj",
    )(h, g, w["w_in"], w["g_q"], w["w_nope"], w["w_rope_a"], w["w_rope_b"], w["g_kv"], w["w_uk_bd"], cos_t, sin_t,
      *stacked)


def _upper_ones(n):
    r = lax.broadcasted_iota(jnp.int32, (2 * n, 2 * n), 0) % n
    c = lax.broadcasted_iota(jnp.int32, (2 * n, 2 * n), 1)
    return jnp.where(jnp.logical_or(r > c, c >= n), 1.0, 0.0).astype(BF16)


def _sb_log_terms(z, upper_ones, visible):
    n = upper_ones.shape[0] // 2
    sp = jnp.log(1.0 + jnp.exp2(jnp.abs(z) * (-LOG2E)))
    log_beta = jnp.minimum(z, 0.0) - sp
    log_keep = log_beta - z
    if visible is not None:
        log_keep = jnp.where(visible, log_keep, 0.0)
    hi = log_keep.astype(BF16)
    lo = (log_keep - hi.astype(F32)).astype(BF16)
    later, total = [], None
    for t in reversed(range(z.shape[1] // n)):
        cols = slice(t * n, (t + 1) * n)
        sums = _dot(jnp.concatenate([hi[:, cols], lo[:, cols]], axis=1), upper_ones)
        later.insert(0, sums[:, :n] if total is None else sums[:, :n] + total)
        total = sums[:, n:] if total is None else total + sums[:, n:]
    return log_beta, later[0] if len(later) == 1 else jnp.concatenate(later, axis=1), total


def _sb_weights(log_beta, later, visible):
    a = jnp.exp(log_beta + later)
    if visible is not None:
        a = jnp.where(visible, a, 0.0)
    return a.astype(BF16)


def _sb_attn_kernel(q_ref, k_ref, v_ref, o_ref, qm_ref, acc_ref, carry_ref):
    qi = pl.program_id(1)
    pairs = SB_HEADS // 2
    rows = SB_HEADS * Q_BLOCK
    upper_ones = _upper_ones(LANES)
    left = lax.broadcasted_iota(jnp.int32, (Q_BLOCK, LANES), 1) < HEAD_DIM
    for p in range(pairs):
        q2 = q_ref[:, p * LANES:(p + 1) * LANES]
        zero = jnp.zeros_like(q2)
        qm_ref[p] = jnp.concatenate([jnp.where(left, q2, zero), jnp.where(left, zero, q2)], axis=0)

    groups = ((0, 1), (2, 3))
    grows = rows // len(groups)

    def block(kb, nblk, first):
        keys = pl.ds(pl.multiple_of(kb * Q_BLOCK, Q_BLOCK), nblk * Q_BLOCK)
        zs = [jnp.concatenate([_dot_nt(qm_ref[p], k_ref[keys, p * LANES:(p + 1) * LANES]) for p in g], axis=0)
              for g in groups]
        visible = None
        if first:
            r = lax.broadcasted_iota(jnp.int32, (grows, LANES), 0) % Q_BLOCK
            visible = lax.broadcasted_iota(jnp.int32, (grows, LANES), 1) < r
        terms = [_sb_log_terms(z, upper_ones, visible) for z in zs]
        for gi, g in enumerate(groups):
            log_beta, later, total = terms[gi]
            grp = slice(gi * grows, (gi + 1) * grows)
            if first:
                carry_ref[grp] = total
            else:
                carry = carry_ref[grp]
                later = later + jnp.concatenate([carry] * nblk, axis=1)
                carry_ref[grp] = carry + total
            a = _sb_weights(log_beta, later, visible)
            for i, p in enumerate(g):
                pv = _dot(a[i * 2 * Q_BLOCK:(i + 1) * 2 * Q_BLOCK], v_ref[keys, p * LANES:(p + 1) * LANES])
                acc_ref[p] = pv if first else acc_ref[p] + pv

    block(qi, 1, True)

    @pl.when(qi % 2 == 1)
    def _():
        block(qi - 1, 1, False)

    top = qi - qi % 2

    @pl.loop(0, qi // 2)
    def _(j):
        block(top - 2 - 2 * j, 2, False)

    for p in range(pairs):
        o_ref[:, p * LANES:(p + 1) * LANES] = jnp.where(left, acc_ref[p, :Q_BLOCK], acc_ref[p, Q_BLOCK:])


def _sb_attn(q, k16, v16, batch, seq):
    nq = seq // Q_BLOCK
    return pl.pallas_call(
        _sb_attn_kernel, grid=(batch, nq),
        in_specs=[pl.BlockSpec((Q_BLOCK, SB_WIDTH), lambda b, i: (b * nq + i, 0)),
                  pl.BlockSpec((seq, SB_WIDTH), lambda b, i: (b, 0)),
                  pl.BlockSpec((seq, SB_WIDTH), lambda b, i: (b, 0))],
        out_specs=pl.BlockSpec((Q_BLOCK, SB_WIDTH), lambda b, i: (b * nq + i, 0)),
        out_shape=jax.ShapeDtypeStruct((batch * seq, SB_WIDTH), F32),
        scratch_shapes=[pltpu.VMEM((SB_HEADS // 2, 2 * Q_BLOCK, LANES), BF16),
                        pltpu.VMEM((SB_HEADS // 2, 2 * Q_BLOCK, LANES), F32),
                        pltpu.VMEM((SB_HEADS * Q_BLOCK, LANES), F32)],
        compiler_params=_params(("parallel", "parallel")), name="sb_attn",
    )(q, k16, v16)


def _mla_attn_kernel(q_ref, kv_ref, latt_ref, o_ref, m_ref, l_ref, acc_ref):
    qi = pl.program_id(1)
    cols = MLA_HEADS * Q_BLOCK
    q = q_ref[0].reshape(cols, QCAT)
    key = lax.broadcasted_iota(jnp.int32, (Q_BLOCK, cols), 0)
    qry = lax.broadcasted_iota(jnp.int32, (Q_BLOCK, cols), 1) % Q_BLOCK

    def scores(start, size):
        kv = kv_ref[pl.ds(pl.multiple_of(start, Q_BLOCK), size), :]
        return _dot_nt(kv, q) * (MLA_SCALE * LOG2E)

    s = jnp.where(key <= qry, scores(qi * Q_BLOCK, Q_BLOCK), NEG_INF)
    m = jnp.max(s, axis=0, keepdims=True)
    p = jnp.exp2(s - m)
    m_ref[...] = m
    l_ref[...] = jnp.sum(p, axis=0, keepdims=True)
    acc_ref[...] = _dot(latt_ref[qi], p.astype(BF16))

    halves = (slice(0, cols // 2), slice(cols // 2, cols))

    def update(start, size, latt):
        kv = kv_ref[pl.ds(pl.multiple_of(start, Q_BLOCK), size), :]
        ss = [_dot_nt(kv, q[h]) * (MLA_SCALE * LOG2E) for h in halves]
        for h, s in zip(halves, ss):
            m_prev = m_ref[:, h]
            m_new = jnp.maximum(m_prev, jnp.max(s, axis=0, keepdims=True))
            alpha = jnp.exp2(m_prev - m_new)
            p = jnp.exp2(s - m_new)
            l_ref[:, h] = alpha * l_ref[:, h] + jnp.sum(p, axis=0, keepdims=True)
            acc_ref[:, h] = alpha * acc_ref[:, h] + _dot(latt, p.astype(BF16))
            m_ref[:, h] = m_new

    @pl.when(qi % 2 == 1)
    def _():
        update((qi - 1) * Q_BLOCK, Q_BLOCK, latt_ref[qi - 1])

    @pl.loop(0, qi // 2)
    def _(j):
        latt = jnp.concatenate([latt_ref[2 * j], latt_ref[2 * j + 1]], axis=1)
        update(j * 2 * Q_BLOCK, 2 * Q_BLOCK, latt)

    out = acc_ref[...] / l_ref[...]
    for hd in range(MLA_HEADS):
        blk = slice(hd * Q_BLOCK, (hd + 1) * Q_BLOCK)
        o_ref[:, hd * KV_RANK:(hd + 1) * KV_RANK] = out[:, blk].T.astype(BF16)


def _mla_attn(qcat, kvcat, latt, batch, seq):
    nq = seq // Q_BLOCK
    cols = MLA_HEADS * Q_BLOCK
    return pl.pallas_call(
        _mla_attn_kernel, grid=(batch, nq),
        in_specs=[pl.BlockSpec((1, MLA_HEADS, Q_BLOCK, QCAT), lambda b, i: (b * nq + i, 0, 0, 0)),
                  pl.BlockSpec((seq, QCAT), lambda b, i: (b, 0)),
                  pl.BlockSpec((nq, KV_RANK, Q_BLOCK), lambda b, i: (b, 0, 0))],
        out_specs=pl.BlockSpec((Q_BLOCK, MLA_HEADS * KV_RANK), lambda b, i: (b * nq + i, 0)),
        out_shape=jax.ShapeDtypeStruct((batch * seq, MLA_HEADS * KV_RANK), BF16),
        scratch_shapes=[pltpu.VMEM((1, cols), F32), pltpu.VMEM((1, cols), F32), pltpu.VMEM((KV_RANK, cols), F32)],
        compiler_params=_params(("parallel", "parallel")), name="mla_attn",
    )(qcat, kvcat, latt)


def _decode_kernel(pt_ref, qsb_ref, knew_ref, vnew_ref, qcat_ref, latnew_ref, krnew_ref, *refs):
    n = PAGES_PER_STEP
    kt_refs, vt_refs, lat_refs, krt_refs = refs[:n], refs[n:2 * n], refs[2 * n:3 * n], refs[3 * n:4 * n]
    osb_ref, omla_ref = refs[4 * n:4 * n + 2]
    acc_sb, carry_ref, m_ref, l_ref, acc_mla = refs[4 * n + 2:]
    step = pl.program_id(1)
    nq = qsb_ref.shape[0]
    rows = SB_HEADS * nq

    q_tiled = jnp.concatenate([qsb_ref[...]] * SB_HEADS, axis=0)
    row_head = lax.broadcasted_iota(jnp.int32, (rows, SB_WIDTH), 0) // nq
    lane_head = lax.broadcasted_iota(jnp.int32, (rows, SB_WIDTH), 1) // HEAD_DIM
    own_head = row_head == lane_head
    q_bd = jnp.where(own_head, q_tiled, 0.0).astype(BF16)
    qcat = qcat_ref[...].astype(BF16)
    q_lat = qcat[:, :KV_RANK]
    q_rope = qcat[:, KV_RANK:KV_RANK + MLA_ROPE]
    upper_ones = _upper_ones(PAGE)
    scale = MLA_SCALE * LOG2E

    @pl.when(step == 0)
    def _():
        pad = PAGE - nq
        q_idx = lax.broadcasted_iota(jnp.int32, (rows, PAGE), 0) % nq
        key_idx = lax.broadcasted_iota(jnp.int32, (rows, PAGE), 1)
        knew = jnp.concatenate([knew_ref[...], jnp.zeros((pad, SB_WIDTH), F32)], axis=0).astype(BF16)
        vnew = jnp.concatenate([vnew_ref[...], jnp.zeros((pad, SB_WIDTH), F32)], axis=0).astype(BF16)
        visible = key_idx < q_idx
        log_beta, later, total = _sb_log_terms(_dot_nt(q_bd, knew), upper_ones, visible)
        carry_ref[...] = total
        acc_sb[...] = _dot(_sb_weights(log_beta, later, visible), vnew)
        lat = jnp.concatenate([latnew_ref[...], jnp.zeros((pad, KV_RANK), F32)], axis=0).astype(BF16)
        kr = jnp.concatenate([krnew_ref[...], jnp.zeros((pad, MLA_ROPE), F32)], axis=0).astype(BF16)
        s = jnp.where(key_idx <= q_idx, (_dot_nt(q_lat, lat) + _dot_nt(q_rope, kr)) * scale, NEG_INF)
        m = jnp.max(s, axis=1, keepdims=True)
        p = jnp.exp2(s - m)
        m_ref[...] = m
        l_ref[...] = jnp.sum(p, axis=1, keepdims=True)
        acc_mla[...] = _dot(p.astype(BF16), lat)

    halves = [range(g, g + PAGE_GROUP) for g in range(0, n, PAGE_GROUP)]
    zs = [jnp.concatenate([_dot(q_bd, kt_refs[j][...].astype(BF16)) for j in half], axis=0) for half in halves]
    lats = [lat_refs[j][...].astype(BF16) for j in range(n)]
    s = jnp.concatenate([_dot_nt(q_lat, lats[j]) + _dot(q_rope, krt_refs[j][...].astype(BF16)) for j in range(n)],
                        axis=1) * scale
    terms = [_sb_log_terms(z, upper_ones, None) for z in zs]
    carry = carry_ref[...]
    acc = acc_sb[...]
    for half, (log_beta, later, total) in zip(halves, terms):
        for i, j in enumerate(half):
            page = slice(i * rows, (i + 1) * rows)
            a = _sb_weights(log_beta[page], later[page] + carry, None)
            carry = carry + total[page]
            acc = acc + _dot_nt(a, vt_refs[j][...].astype(BF16))
    carry_ref[...] = carry
    acc_sb[...] = acc

    m = m_ref[...]
    m_new = jnp.maximum(m, jnp.max(s, axis=1, keepdims=True))
    alpha = jnp.exp2(m - m_new)
    p = jnp.exp2(s - m_new)
    l = alpha * l_ref[...] + jnp.sum(p, axis=1, keepdims=True)
    p = p.astype(BF16)
    accm = alpha * acc_mla[...]
    for j in range(n):
        accm = accm + _dot(p[:, j * PAGE:(j + 1) * PAGE], lats[j])
    m_ref[...] = m_new
    l_ref[...] = l
    acc_mla[...] = accm

    @pl.when(step == pl.num_programs(1) - 1)
    def _():
        own = jnp.where(own_head, acc, 0.0)
        out = own[:nq]
        for hd in range(1, SB_HEADS):
            out = out + own[hd * nq:(hd + 1) * nq]
        osb_ref[...] = out
        omla_ref[...] = accm / l


def _decode(layer, page_table, qsb, knew, vnew, qcat, latnew, krnew, ckt, cvt, clat, ckrt):
    nb, nq, _ = qsb.shape
    n_pages = page_table.shape[1]
    n = PAGES_PER_STEP
    steps = n_pages // n
    rows = SB_HEADS * nq

    def per_seq(shape):
        return pl.BlockSpec((None,) + shape, lambda b, s, pt: (b, 0, 0))

    def page_spec(shape, j):
        return pl.BlockSpec((None, None) + shape,
                            lambda b, s, pt, j=j: (layer, pt[b, n_pages - 1 - (s * n + j)], 0, 0))

    in_specs = [per_seq((nq, SB_WIDTH)), per_seq((nq, SB_WIDTH)), per_seq((nq, SB_WIDTH)), per_seq((rows, QCAT)),
                per_seq((nq, KV_RANK)), per_seq((nq, MLA_ROPE))]
    for shape in ((SB_WIDTH, PAGE), (SB_WIDTH, PAGE), (PAGE, KV_RANK), (MLA_ROPE, PAGE)):
        in_specs += [page_spec(shape, j) for j in range(n)]
    grid_spec = pltpu.PrefetchScalarGridSpec(
        num_scalar_prefetch=1, grid=(nb, steps), in_specs=in_specs,
        out_specs=(per_seq((nq, SB_WIDTH)), per_seq((rows, KV_RANK))),
        scratch_shapes=[pltpu.VMEM((rows, SB_WIDTH), F32), pltpu.VMEM((rows, PAGE), F32), pltpu.VMEM((rows, 1), F32),
                        pltpu.VMEM((rows, 1), F32), pltpu.VMEM((rows, KV_RANK), F32)])
    return pl.pallas_call(
        _decode_kernel, grid_spec=grid_spec,
        out_shape=(jax.ShapeDtypeStruct((nb, nq, SB_WIDTH), F32), jax.ShapeDtypeStruct((nb, rows, KV_RANK), F32)),
        compiler_params=_params(("parallel", "arbitrary")), name="decode_attn",
    )(page_table, qsb, knew, vnew, qcat, latnew, krnew, *([ckt] * n), *([cvt] * n), *([clat] * n), *([ckrt] * n))


def _post_kernel(osb_ref, omla_ref, h_ref, gsb_ref, gmla_ref, wuv_ref, wout_ref, gffn_ref, h1_ref, hn_ref):
    pairs = [_dot(omla_ref[:, p * 2 * KV_RANK:(p + 1) * 2 * KV_RANK], wuv_ref[p]) for p in range(MLA_HEADS // 2)]
    o_mla = jnp.concatenate(pairs, axis=1)
    mixed = jnp.concatenate([_rms(osb_ref[...], gsb_ref[...]), _rms(o_mla, gmla_ref[...])], axis=1).astype(BF16)
    h1 = h_ref[...] + _dot(mixed, wout_ref[...])
    h1_ref[...] = h1
    hn_ref[...] = _rms(h1, gffn_ref[...]).astype(BF16)


def _post(osb, omla, h, w, tm):
    t = h.shape[0]
    row = lambda n: pl.BlockSpec((tm, n), lambda i: (i, 0))
    return pl.pallas_call(
        _post_kernel, grid=(t // tm,),
        in_specs=[row(SB_WIDTH), row(MLA_HEADS * KV_RANK), row(D_MODEL), _full((1, SB_WIDTH)), _full((1, MLA_WIDTH)),
                  _full((MLA_HEADS // 2, 2 * KV_RANK, 2 * MLA_V)), _full((D_MODEL, D_MODEL)), _full((1, D_MODEL))],
        out_specs=(row(D_MODEL), row(D_MODEL)),
        out_shape=(jax.ShapeDtypeStruct((t, D_MODEL), F32), jax.ShapeDtypeStruct((t, D_MODEL), BF16)),
        compiler_params=_params(("parallel",)), name="post_attn",
    )(osb, omla, h, w["g_out_sb"], w["g_out_mla"], w["w_uv_bd"], w["w_out"], w["g_ffn"])


def _swiglu_chunk(x, wg, wu, wd):
    g = _dot(x, wg)
    act = (g * jax.nn.sigmoid(g) * _dot(x, wu)).astype(BF16)
    return _dot(act, wd)


def _ffn_kernel(hn_ref, h1_ref, wg_ref, wu_ref, wd_ref, gfin_ref, o_ref, acc_ref, *, final_norm):
    j = pl.program_id(1)

    @pl.when(j == 0)
    def _():
        acc_ref[...] = jnp.zeros_like(acc_ref)

    acc_ref[...] += _swiglu_chunk(hn_ref[...], wg_ref[...], wu_ref[...], wd_ref[...])

    @pl.when(j == pl.num_programs(1) - 1)
    def _():
        out = h1_ref[...] + acc_ref[...]
        o_ref[...] = _rms(out, gfin_ref[...]) if final_norm else out


def _ffn(hn, h1, wg, wu, wd, g_final, tm, tf, final_norm):
    t = hn.shape[0]
    f = wg.shape[1]
    return pl.pallas_call(
        functools.partial(_ffn_kernel, final_norm=final_norm), grid=(t // tm, f // tf),
        in_specs=[pl.BlockSpec((tm, D_MODEL), lambda i, j: (i, 0)), pl.BlockSpec((tm, D_MODEL), lambda i, j: (i, 0)),
                  pl.BlockSpec((D_MODEL, tf), lambda i, j: (0, j)), pl.BlockSpec((D_MODEL, tf), lambda i, j: (0, j)),
                  pl.BlockSpec((tf, D_MODEL), lambda i, j: (j, 0)), _full((1, D_MODEL))],
        out_specs=pl.BlockSpec((tm, D_MODEL), lambda i, j: (i, 0)),
        out_shape=jax.ShapeDtypeStruct((t, D_MODEL), F32),
        scratch_shapes=[pltpu.VMEM((tm, D_MODEL), F32)],
        compiler_params=_params(("parallel", "arbitrary")), name="ffn_dense",
    )(hn, h1, wg, wu, wd, g_final)


def _router_kernel(h1_ref, gffn_ref, wr_ref, comb_ref, meta_ref):
    hn = _rms(h1_ref[...], gffn_ref[...])
    w = wr_ref[...]
    hn_hi, w_hi = hn.astype(BF16), w.astype(BF16)
    hn_lo, w_lo = (hn - hn_hi.astype(F32)).astype(BF16), (w - w_hi.astype(F32)).astype(BF16)
    logits = _dot(hn_hi, w_hi) + _dot(hn_lo, w_hi) + _dot(hn_hi, w_lo)
    lane = lax.broadcasted_iota(jnp.int32, logits.shape, 1)
    logits = jnp.where(lane < N_EXPERTS, logits, NEG_INF)
    m1 = jnp.max(logits, axis=1, keepdims=True)
    i1 = jnp.min(jnp.where(logits == m1, lane, LANES), axis=1, keepdims=True)
    rest = jnp.where(lane == i1, NEG_INF, logits)
    m2 = jnp.max(rest, axis=1, keepdims=True)
    i2 = jnp.min(jnp.where(rest == m2, lane, LANES), axis=1, keepdims=True)
    e2 = jnp.exp(m2 - m1)
    g1 = 1.0 / (1.0 + e2)
    g2 = e2 / (1.0 + e2)
    comb_ref[...] = jnp.where(lane == i1, g1, 0.0) + jnp.where(lane == i2, g2, 0.0)
    meta_ref[...] = jnp.where(lane == 0, i1.astype(F32), jnp.where(lane == 1, i2.astype(F32),
                              jnp.where(lane == 2, g1, jnp.where(lane == 3, g2, 0.0))))


def _router(h1, g_ffn, w_router, tm):
    t = h1.shape[0]
    return pl.pallas_call(
        _router_kernel, grid=(t // tm,),
        in_specs=[pl.BlockSpec((tm, D_MODEL), lambda i: (i, 0)), _full((1, D_MODEL)), _full((D_MODEL, LANES))],
        out_specs=(pl.BlockSpec((tm, LANES), lambda i: (i, 0)), pl.BlockSpec((tm, LANES), lambda i: (i, 0))),
        out_shape=(jax.ShapeDtypeStruct((t, LANES), F32), jax.ShapeDtypeStruct((t, LANES), F32)),
        compiler_params=_params(("parallel",)), name="router",
    )(h1, g_ffn, w_router)


def _moe_kernel(hn_ref, h1_ref, comb_ref, wg_ref, wu_ref, wd_ref, gfin_ref, o_ref, acc_ref, *, final_norm):
    e = pl.program_id(1)
    j = pl.program_id(2)

    @pl.when(jnp.logical_and(e == 0, j == 0))
    def _():
        acc_ref[...] = jnp.zeros_like(acc_ref)

    lane = lax.broadcasted_iota(jnp.int32, comb_ref.shape, 1)
    gate = jnp.sum(jnp.where(lane == e, comb_ref[...], 0.0), axis=1, keepdims=True)
    acc_ref[...] += gate * _swiglu_chunk(hn_ref[...], wg_ref[...], wu_ref[...], wd_ref[...])

    @pl.when(jnp.logical_and(e == pl.num_programs(1) - 1, j == pl.num_programs(2) - 1))
    def _():
        out = h1_ref[...] + acc_ref[...]
        o_ref[...] = _rms(out, gfin_ref[...]) if final_norm else out


def _moe(hn, h1, comb, wg, wu, wd, g_final, tm, tf, final_norm):
    t = hn.shape[0]
    ne, _, f = wg.shape
    tok = lambda n: pl.BlockSpec((tm, n), lambda i, e, j: (i, 0))
    return pl.pallas_call(
        functools.partial(_moe_kernel, final_norm=final_norm), grid=(t // tm, ne, f // tf),
        in_specs=[tok(D_MODEL), tok(D_MODEL), tok(LANES),
                  pl.BlockSpec((None, D_MODEL, tf), lambda i, e, j: (e, 0, j)),
                  pl.BlockSpec((None, D_MODEL, tf), lambda i, e, j: (e, 0, j)),
                  pl.BlockSpec((None, tf, D_MODEL), lambda i, e, j: (e, j, 0)), _full((1, D_MODEL))],
        out_specs=tok(D_MODEL),
        out_shape=jax.ShapeDtypeStruct((t, D_MODEL), F32),
        scratch_shapes=[pltpu.VMEM((tm, D_MODEL), F32)],
        compiler_params=_params(("parallel", "arbitrary", "arbitrary")), name="ffn_moe",
    )(hn, h1, comb, wg, wu, wd, g_final)


def _routing_tables(e1, e2, n_tiles):
    t = e1.shape[0]
    tr = MOE_ROWS
    flat = jnp.stack([e1, e2], axis=1).reshape(2 * t)
    order = jnp.argsort(flat, stable=True).astype(jnp.int32)
    counts = jnp.sum(flat[:, None] == jnp.arange(N_EXPERTS, dtype=jnp.int32)[None, :], axis=0).astype(jnp.int32)
    padded = (counts + tr - 1) // tr * tr
    pend = jnp.cumsum(padded)
    pstart = pend - padded
    ustart = jnp.cumsum(counts) - counts
    n_valid = (pend[-1] // tr).astype(jnp.int32)
    tile_start = jnp.arange(n_tiles, dtype=jnp.int32) * tr
    tile_expert = jnp.minimum(jnp.sum(tile_start[:, None] >= pend[None, :], axis=1), N_EXPERTS - 1).astype(jnp.int32)
    slot = jnp.arange(n_tiles * tr, dtype=jnp.int32)
    e = jnp.repeat(tile_expert, tr)
    rank = slot - pstart[e]
    valid = jnp.logical_and(slot < pend[-1], rank < counts[e])
    a = order[jnp.clip(ustart[e] + rank, 0, 2 * t - 1)]
    src = jnp.where(valid, a // 2, 0)
    dst = jnp.where(valid, (a % 2) * t + a // 2, 2 * t + e * tr + jnp.clip(rank - counts[e], 0, tr - 1))
    dst = jnp.concatenate([2 * t + N_EXPERTS * tr + jnp.arange(tr, dtype=jnp.int32), dst])
    return src.reshape(n_tiles, 1, tr), dst.reshape(n_tiles + 1, 1, tr), tile_expert, n_valid.reshape(1)


def _moe_grouped_kernel(te_ref, nv_ref, src_cur, src_next, dst_prev, dst_cur, h1_hbm, gffn_ref, wg_ref, wu_ref, wd_ref,
                        y_hbm, xbuf, xn_ref, acc_ref, obuf, gsem, ssem):
    i = pl.program_id(0)
    j = pl.program_id(1)
    nj = pl.num_programs(1)
    nv = nv_ref[0]
    tr = xn_ref.shape[0]
    ch = tr // MOE_CHUNKS
    slot = i % 2
    other = 1 - slot

    def gather_row(tok_ref, r, s):
        return pltpu.make_async_copy(h1_hbm.at[pl.ds(tok_ref[0, r], 1)], xbuf.at[s, pl.ds(r, 1)], gsem.at[s])

    def scatter_row(row_ref, r, s):
        return pltpu.make_async_copy(obuf.at[s, pl.ds(r, 1)], y_hbm.at[pl.ds(row_ref[0, r], 1)], ssem.at[s])

    @pl.when(i < nv)
    def _():
        @pl.when(jnp.logical_and(i == 0, j == 0))
        def _():
            obuf[...] = jnp.zeros_like(obuf)
            for r in range(tr):
                gather_row(src_cur, r, 0).start()
            for r in range(tr):
                gather_row(src_cur, r, 0).wait()

        @pl.when(j == 0)
        def _():
            xn_ref[...] = _rms(xbuf[slot], gffn_ref[...]).astype(BF16)
            acc_ref[...] = jnp.zeros_like(acc_ref)

        for c in range(ch):
            gather_row(src_next, j * ch + c, other).start()
        for c in range(ch):
            scatter_row(dst_prev, j * ch + c, other).start(priority=c % 2)

        acc_ref[...] += _swiglu_chunk(xn_ref[...], wg_ref[...], wu_ref[...], wd_ref[...])

        @pl.when(j == nj - 1)
        def _():
            obuf[slot] = acc_ref[...]
            for r in range(tr):
                gather_row(src_next, r, other).wait()
            for r in range(tr):
                scatter_row(dst_prev, r, other).wait()

            @pl.when(i == nv - 1)
            def _():
                for r in range(tr):
                    scatter_row(dst_cur, r, slot).start(priority=r % 2)
                for r in range(tr):
                    scatter_row(dst_cur, r, slot).wait()


def _moe_grouped(h1, g_ffn, src, dst, tile_expert, n_valid, wg, wu, wd):
    t = h1.shape[0]
    n_tiles, _, tr = src.shape
    ne, _, f = wg.shape
    tf = f // MOE_CHUNKS
    last = n_tiles - 1
    smem = functools.partial(pl.BlockSpec, memory_space=pltpu.SMEM)
    in_specs = [
        smem((None, 1, tr), lambda i, j, te, nv: (i, 0, 0)),
        smem((None, 1, tr), lambda i, j, te, nv: (jnp.minimum(i + 1, last), 0, 0)),
        smem((None, 1, tr), lambda i, j, te, nv: (i, 0, 0)),
        smem((None, 1, tr), lambda i, j, te, nv: (i + 1, 0, 0)),
        pl.BlockSpec(memory_space=pl.ANY),
        pl.BlockSpec((1, D_MODEL), lambda i, j, te, nv: (0, 0)),
        pl.BlockSpec((None, D_MODEL, tf), lambda i, j, te, nv: (te[i], 0, j)),
        pl.BlockSpec((None, D_MODEL, tf), lambda i, j, te, nv: (te[i], 0, j)),
        pl.BlockSpec((None, tf, D_MODEL), lambda i, j, te, nv: (te[i], j, 0)),
    ]
    grid_spec = pltpu.PrefetchScalarGridSpec(
        num_scalar_prefetch=2, grid=(n_tiles, MOE_CHUNKS), in_specs=in_specs,
        out_specs=pl.BlockSpec(memory_space=pl.ANY),
        scratch_shapes=[pltpu.VMEM((2, tr, D_MODEL), F32), pltpu.VMEM((tr, D_MODEL), BF16),
                        pltpu.VMEM((tr, D_MODEL), F32), pltpu.VMEM((2, tr, D_MODEL), F32),
                        pltpu.SemaphoreType.DMA((2,)), pltpu.SemaphoreType.DMA((2,))])
    return pl.pallas_call(
        _moe_grouped_kernel, grid_spec=grid_spec,
        out_shape=jax.ShapeDtypeStruct((2 * t + (ne + 1) * tr, D_MODEL), F32),
        compiler_params=_params(("arbitrary", "arbitrary")), name="ffn_moe_grouped",
    )(tile_expert, n_valid, src, src, dst, dst, h1, g_ffn, wg, wu, wd)


def _moe_combine_kernel(h1_ref, y1_ref, y2_ref, meta_ref, gfin_ref, o_ref, *, final_norm):
    meta = meta_ref[...]
    lane = lax.broadcasted_iota(jnp.int32, meta.shape, 1)
    g1 = jnp.sum(jnp.where(lane == 2, meta, 0.0), axis=1, keepdims=True)
    g2 = jnp.sum(jnp.where(lane == 3, meta, 0.0), axis=1, keepdims=True)
    out = h1_ref[...] + g1 * y1_ref[...] + g2 * y2_ref[...]
    o_ref[...] = _rms(out, gfin_ref[...]) if final_norm else out


def _moe_combine(h1, y, meta, g_final, tm, final_norm):
    t = h1.shape[0]
    nt = t // tm
    return pl.pallas_call(
        functools.partial(_moe_combine_kernel, final_norm=final_norm), grid=(nt,),
        in_specs=[pl.BlockSpec((tm, D_MODEL), lambda i: (i, 0)), pl.BlockSpec((tm, D_MODEL), lambda i: (i, 0)),
                  pl.BlockSpec((tm, D_MODEL), lambda i: (i + nt, 0)), pl.BlockSpec((tm, LANES), lambda i: (i, 0)),
                  _full((1, D_MODEL))],
        out_specs=pl.BlockSpec((tm, D_MODEL), lambda i: (i, 0)),
        out_shape=jax.ShapeDtypeStruct((t, D_MODEL), F32),
        compiler_params=_params(("parallel",)), name="moe_combine",
    )(h1, y, y, meta, g_final)


def _moe_routed(h1, g_ffn, w_router, wg, wu, wd, g_final, final_norm):
    t = h1.shape[0]
    _, meta = _router(h1, g_ffn, w_router, 1024)
    e1 = meta[:, 0].astype(jnp.int32)
    e2 = meta[:, 1].astype(jnp.int32)
    n_tiles = -(-2 * t // MOE_ROWS) + N_EXPERTS
    src, dst, tile_expert, n_valid = _routing_tables(e1, e2, n_tiles)
    y = _moe_grouped(h1, g_ffn, src, dst, tile_expert, n_valid, wg, wu, wd)
    return _moe_combine(h1, y, meta, g_final, 512, final_norm)


def _prep_layer(l, g_attn_norm, w_in, g_q_norm, w_uq, g_kv_norm, w_uk, w_uv, g_out_sb, g_out_mla, w_out, g_ffn_norm):
    half = MLA_ROPE // 2
    o = 3 * SB_WIDTH + Q_RANK + KV_RANK
    kr = w_in[l][:, o:]
    pad = jnp.zeros((D_MODEL, LANES - MLA_ROPE), F32)
    kr_a = jnp.concatenate([kr, pad], axis=1)
    kr_b = jnp.concatenate([-kr[:, half:], kr[:, :half], pad], axis=1)
    w_in_p = jnp.concatenate([w_in[l][:, :o], kr_a, kr_b], axis=1).astype(BF16)

    uq = w_uq[l]
    w_nope = uq[:, :, :MLA_NOPE].reshape(Q_RANK, MLA_HEADS * MLA_NOPE).astype(BF16)
    qr = uq[:, :, MLA_NOPE:]
    hpad = jnp.zeros((Q_RANK, MLA_HEADS, LANES - MLA_ROPE), F32)
    w_rope_a = jnp.concatenate([qr, hpad], axis=2).reshape(Q_RANK, MLA_HEADS * LANES).astype(BF16)
    w_rope_b = jnp.concatenate([-qr[:, :, half:], qr[:, :, :half], hpad], axis=2)
    w_rope_b = w_rope_b.reshape(Q_RANK, MLA_HEADS * LANES).astype(BF16)

    uk = jnp.transpose(w_uk[l], (1, 2, 0)).reshape(MLA_HEADS // 2, 2, MLA_NOPE, KV_RANK)
    zk = jnp.zeros((MLA_HEADS // 2, MLA_NOPE, KV_RANK), F32)
    w_uk_bd = jnp.concatenate([jnp.concatenate([uk[:, 0], zk], axis=2),
                               jnp.concatenate([zk, uk[:, 1]], axis=2)], axis=1).astype(BF16)
    uv = jnp.transpose(w_uv[l], (1, 0, 2)).reshape(MLA_HEADS // 2, 2, KV_RANK, MLA_V)
    zv = jnp.zeros((MLA_HEADS // 2, KV_RANK, MLA_V), F32)
    w_uv_bd = jnp.concatenate([jnp.concatenate([uv[:, 0], zv], axis=2),
                               jnp.concatenate([zv, uv[:, 1]], axis=2)], axis=1).astype(BF16)
    return {
        "g_attn": g_attn_norm[l][None], "w_in": w_in_p, "g_q": g_q_norm[l][None], "w_nope": w_nope,
        "w_rope_a": w_rope_a, "w_rope_b": w_rope_b, "g_kv": g_kv_norm[l][None], "w_uk_bd": w_uk_bd,
        "w_uv_bd": w_uv_bd, "g_out_sb": g_out_sb[l][None], "g_out_mla": g_out_mla[l][None],
        "w_out": w_out[l].astype(BF16), "g_ffn": g_ffn_norm[l][None],
    }


def _rope_tables(pos):
    inv_freq = 1.0 / (ROPE_THETA ** (jnp.arange(0, MLA_ROPE, 2, dtype=F32) / MLA_ROPE))
    ang = pos.astype(F32)[:, None] * inv_freq[None, :]
    pad = jnp.zeros((pos.shape[0], LANES - MLA_ROPE), F32)
    cos, sin = jnp.cos(ang), jnp.sin(ang)
    return jnp.concatenate([cos, cos, pad], axis=1), jnp.concatenate([sin, sin, pad], axis=1)


def kernel(x_prompt, x_sample, cache_sb_k, cache_sb_v, cache_mla_latent, cache_mla_krope, page_table, g_attn_norm,
           w_in, g_q_norm, w_uq, g_kv_norm, w_uk, w_uv, g_out_sb, g_out_mla, w_out, g_ffn_norm, w_gate_dense,
           w_up_dense, w_down_dense, w_router, w_gate_moe, w_up_moe, w_down_moe, g_final):
    batch, seq, _ = x_prompt.shape
    nb, nq, _ = x_sample.shape
    depth, n_pool = cache_sb_k.shape[:2]
    past_len = page_table.shape[1] * PAGE
    tp, ts = batch * seq, nb * nq

    ckt = jnp.transpose(cache_sb_k, (0, 1, 3, 4, 2)).reshape(depth, n_pool, SB_WIDTH, PAGE)
    cvt = jnp.transpose(cache_sb_v, (0, 1, 3, 4, 2)).reshape(depth, n_pool, SB_WIDTH, PAGE)
    ckrt = jnp.transpose(cache_mla_krope, (0, 1, 3, 2))
    cos_p, sin_p = _rope_tables(jnp.tile(jnp.arange(seq, dtype=jnp.int32), batch))
    cos_s, sin_s = _rope_tables(jnp.tile(past_len + jnp.arange(nq, dtype=jnp.int32), nb))
    g_fin = g_final[None]

    h_p = x_prompt.reshape(tp, D_MODEL)
    h_s = x_sample.reshape(ts, D_MODEL)
    rows_p, rows_s = [], []
    kv_stacked = ()
    for l in range(depth):
        w = _prep_layer(l, g_attn_norm, w_in, g_q_norm, w_uq, g_kv_norm, w_uk, w_uv, g_out_sb, g_out_mla, w_out,
                        g_ffn_norm)
        last = l == depth - 1

        q, k_all, v_all, k16, v16, lat, kr, kvcat, latt, qcat = _proj(h_p, w["g_attn"], w, cos_p, sin_p, 512, l, depth,
                                                                      kv_stacked, seq)
        kv_stacked = (k_all, v_all)
        rows_p.append((lat, kr))
        o_sb = _sb_attn(q, k16, v16, batch, seq)
        o_mla = _mla_attn(qcat, kvcat, latt, batch, seq)
        h1_p, hn_p = _post(o_sb, o_mla, h_p, w, 512)

        q, k, v, _, _, lat, kr, _, _, qcat = _proj(h_s, w["g_attn"], w, cos_s, sin_s, ts)
        rows_s.append((k, v, lat, kr))
        qcat_s = qcat.reshape(ts // Q_BLOCK, MLA_HEADS, Q_BLOCK // nq, nq, QCAT)
        qcat_s = jnp.transpose(qcat_s, (0, 2, 1, 3, 4)).reshape(nb, MLA_HEADS * nq, QCAT).astype(F32)
        o_sb, o_mla = _decode(l, page_table, q.astype(F32).reshape(nb, nq, SB_WIDTH), k.reshape(nb, nq, SB_WIDTH),
                              v.reshape(nb, nq, SB_WIDTH), qcat_s, lat.reshape(nb, nq, KV_RANK),
                              kr.reshape(nb, nq, MLA_ROPE), ckt, cvt, cache_mla_latent, ckrt)
        o_mla = jnp.transpose(o_mla.reshape(nb, MLA_HEADS, nq, KV_RANK), (0, 2, 1, 3))
        o_mla = o_mla.reshape(ts, MLA_HEADS * KV_RANK).astype(BF16)
        h1_s, hn_s = _post(o_sb.reshape(ts, SB_WIDTH), o_mla, h_s, w, ts)

        i = l // 2
        if l % 2 == 0:
            wg, wu, wd = w_gate_dense[i].astype(BF16), w_up_dense[i].astype(BF16), w_down_dense[i].astype(BF16)
            h_p = _ffn(hn_p, h1_p, wg, wu, wd, g_fin, 1024, 512, last)
            h_s = _ffn(hn_s, h1_s, wg, wu, wd, g_fin, ts, 512, last)
        else:
            wg, wu, wd = w_gate_moe[i].astype(BF16), w_up_moe[i].astype(BF16), w_down_moe[i].astype(BF16)
            wr = jnp.concatenate([w_router[i], jnp.zeros((D_MODEL, LANES - N_EXPERTS), F32)], axis=1)
            h_p = _moe_routed(h1_p, w["g_ffn"], wr, wg, wu, wd, g_fin, last)
            h_s = _moe(hn_s, h1_s, _router(h1_s, w["g_ffn"], wr, ts)[0], wg, wu, wd, g_fin, ts, 1792, last)

    y_prompt = h_p.reshape(batch, seq, D_MODEL)
    y_sample = h_s.reshape(nb, nq, D_MODEL)

    def stack(rows, idx, shape):
        return jnp.stack([r[idx].reshape(shape) for r in rows])

    def key_minor_rows(x):
        return jnp.transpose(x.reshape(depth, batch, SB_HEADS, HEAD_DIM, seq), (0, 1, 4, 2, 3))

    return (y_prompt, y_sample,
            key_minor_rows(kv_stacked[0]), key_minor_rows(kv_stacked[1]),
            stack(rows_p, 0, (batch, seq, KV_RANK)), stack(rows_p, 1, (batch, seq, MLA_ROPE)),
            stack(rows_s, 0, (nb, nq, SB_HEADS, HEAD_DIM)), stack(rows_s, 1, (nb, nq, SB_HEADS, HEAD_DIM)),
            stack(rows_s, 2, (nb, nq, KV_RANK)), stack(rows_s, 3, (nb, nq, MLA_ROPE)))
```
